```python
import math
import jax, jax.numpy as jnp
from jax import lax
import numpy as np

D_MODEL = 1024
BATCH = 16
SEQ = 2048
DEPTH = 4
DEC_BATCH = 16
DEC_SEQ = 16
PAST_LEN = 4096

CHUNK = 64
N_MIXERS = 3
N_A = (DEPTH + 2) // 3
N_B = (DEPTH + 1) // 3
N_C = DEPTH // 3

A_HEAD_DIM = 64
A_HEADS = D_MODEL // (2 * A_HEAD_DIM)
A_Q_BLOCK = 128
ROPE_THETA = 10000.0

B_WIDTH = D_MODEL
B_BLOCKS = 4
B_BLOCK_W = B_WIDTH // B_BLOCKS
B_CONV_W = 4
B_C = 8.0

C_HEADS = 16
C_HEAD_DIM = D_MODEL // C_HEADS
C_LEFT_CHUNKS = 8
C_BAND_PAST = C_LEFT_CHUNKS * CHUNK
C_REL_CLIP = 128

D_FF = 4 * D_MODEL
EPS = 1e-6

kernel_name = "hybrid_streaming_encoder_step"


def rms_norm(x, g):
    xf = x.astype(jnp.float32)
    y = xf * lax.rsqrt(jnp.mean(xf * xf, axis=-1, keepdims=True) + EPS)
    return (y * g.astype(jnp.float32)).astype(x.dtype)


def rope(x, pos):
    half = x.shape[-1] // 2
    inv = ROPE_THETA ** (-jnp.arange(half, dtype=jnp.float32) / half)
    ang = pos.astype(jnp.float32)[:, None] * inv[None, :]
    shape = (1, pos.shape[0]) + (1,) * (x.ndim - 3) + (half,)
    cos = jnp.cos(ang).reshape(shape)
    sin = jnp.sin(ang).reshape(shape)
    x1 = x[..., :half].astype(jnp.float32)
    x2 = x[..., half:].astype(jnp.float32)
    return jnp.concatenate([x1 * cos - x2 * sin, x2 * cos + x1 * sin], axis=-1).astype(x.dtype)


def diff_core(q, k, v, lam, mask):
    s = jnp.einsum("bqhcd,bkhcd->bhcqk", q, k).astype(jnp.float32) * (A_HEAD_DIM ** -0.5)
    if mask is not None:
        s = jnp.where(mask, s, -jnp.inf)
    p = jax.nn.softmax(s, axis=-1)
    w = p[:, :, 0] - lam * p[:, :, 1]
    return jnp.einsum("bhqk,bkhe->bqhe", w.astype(v.dtype), v)


def diff_attn_prompt(q, k, v, lam):
    B, T = q.shape[:2]
    nb = T // A_Q_BLOCK
    qb = jnp.moveaxis(q.reshape(B, nb, A_Q_BLOCK, A_HEADS, 2, A_HEAD_DIM), 1, 0)
    k_chunk = jnp.arange(T) // CHUNK

    def block(args):
        q_blk, b = args
        q_chunk = (b * A_Q_BLOCK + jnp.arange(A_Q_BLOCK)) // CHUNK
        return diff_core(q_blk, k, v, lam, k_chunk[None, :] <= q_chunk[:, None])

    o = lax.map(block, (qb, jnp.arange(nb)))
    return jnp.moveaxis(o, 0, 1).reshape(B, T, A_HEADS, 2 * A_HEAD_DIM)


def diff_project(h, pos, w_in, q_g, k_g):
    B, T, _ = h.shape
    q, k, v = jnp.split(h @ w_in, 3, axis=-1)
    q = rope(rms_norm(q.reshape(B, T, A_HEADS, 2, A_HEAD_DIM), q_g), pos)
    k = rope(rms_norm(k.reshape(B, T, A_HEADS, 2, A_HEAD_DIM), k_g), pos)
    return q, k, v.reshape(B, T, A_HEADS, 2 * A_HEAD_DIM)


def diff_output(o, subln_g, lam_init, w_out):
    B, T = o.shape[:2]
    o = rms_norm(o, subln_g) * (1.0 - lam_init)
    return o.reshape(B, T, -1) @ w_out


def diff_mixer(hp, hs, cache_k, cache_v, w_in, q_g, k_g, lam_p, subln_g, w_out, lam_init):
    past = cache_k.shape[1]
    lp = lam_p.astype(jnp.float32)
    lam = jnp.exp(jnp.sum(lp[0] * lp[1])) - jnp.exp(jnp.sum(lp[2] * lp[3])) + lam_init
    qp, kp, vp = diff_project(hp, jnp.arange(hp.shape[1]), w_in, q_g, k_g)
    yp = diff_output(diff_attn_prompt(qp, kp, vp, lam), subln_g, lam_init, w_out)
    qs, ks, vs = diff_project(hs, past + jnp.arange(hs.shape[1]), w_in, q_g, k_g)
    k_all = jnp.concatenate([cache_k, ks], axis=1)
    v_all = jnp.concatenate([cache_v, vs], axis=1)
    ys = diff_output(diff_core(qs, k_all, v_all, lam, None), subln_g, lam_init, w_out)
    return yp, ys, kp, vp, ks, vs


def linear_scan(a, b, h0):
    b = b.at[:, 0].add(a[:, 0] * h0)

    def combine(l, r):
        return l[0] * r[0], r[0] * l[1] + r[1]

    return lax.associative_scan(combine, (a, b), axis=1)[1]


def rglru_block(h, conv_hist, h0, w_in, b_in, conv_w, conv_b, ga_w, ga_b, gx_w, gx_b, lam, w_out):
    B, T, _ = h.shape
    gate, u = jnp.split(h @ w_in + b_in, 2, axis=-1)
    gate = jax.nn.gelu(gate)
    u_ext = jnp.concatenate([conv_hist.astype(u.dtype), u], axis=1)
    xc = sum((u_ext[:, j:j + T] * conv_w[j] for j in range(B_CONV_W)), conv_b)
    xb = xc.reshape(B, T, B_BLOCKS, B_BLOCK_W)
    r = jax.nn.sigmoid(jnp.einsum("btnc,ncd->btnd", xb, ga_w).reshape(B, T, B_WIDTH) + ga_b).astype(jnp.float32)
    i = jax.nn.sigmoid(jnp.einsum("btnc,ncd->btnd", xb, gx_w).reshape(B, T, B_WIDTH) + gx_b).astype(jnp.float32)
    log_a = -B_C * r * jax.nn.softplus(-lam.astype(jnp.float32))
    a = jnp.exp(log_a)
    b = jnp.sqrt(-jnp.expm1(2.0 * log_a)) * (i * xc.astype(jnp.float32))
    hs = linear_scan(a, b, h0.astype(jnp.float32))
    y = (hs.astype(h.dtype) * gate) @ w_out
    return y, u_ext[:, -(B_CONV_W - 1):], hs[:, -1].astype(h0.dtype)


def rel_bias(table, rel):
    idx = jnp.clip(rel, -C_REL_CLIP, C_REL_CLIP) + C_REL_CLIP
    return jnp.moveaxis(table[idx], -1, 0).astype(jnp.float32)


def band_core(q, k, v, bias, valid):
    s = jnp.einsum("bqhd,bkhd->bhqk", q, k).astype(jnp.float32) * (C_HEAD_DIM ** -0.5) + bias
    if valid is not None:
        s = jnp.where(valid, s, -jnp.inf)
    p = jax.nn.softmax(s, axis=-1)
    return jnp.einsum("bhqk,bkhd->bqhd", p.astype(v.dtype), v)


def band_attn_prompt(q, k, v, table):
    B, T = q.shape[:2]
    nc = T // CHUNK
    band = C_BAND_PAST + CHUNK
    pad = ((0, 0), (C_BAND_PAST, 0), (0, 0), (0, 0))
    kp = jnp.pad(k, pad)
    vp = jnp.pad(v, pad)
    qc = jnp.moveaxis(q.reshape(B, nc, CHUNK, C_HEADS, C_HEAD_DIM), 1, 0)
    key_off = jnp.arange(band) - C_BAND_PAST
    bias = rel_bias(table, jnp.arange(CHUNK)[:, None] - key_off[None, :])

    def one_chunk(args):
        q_c, c = args
        start = c * CHUNK
        kb = lax.dynamic_slice_in_dim(kp, start, band, axis=1)
        vb = lax.dynamic_slice_in_dim(vp, start, band, axis=1)
        valid = (start + key_off) >= 0
        return band_core(q_c, kb, vb, bias, valid)

    o = lax.map(one_chunk, (qc, jnp.arange(nc)))
    return jnp.moveaxis(o, 0, 1).reshape(B, T, C_HEADS * C_HEAD_DIM)


def band_mixer(hp, hs, cache_k, cache_v, w_in, q_g, k_g, table, w_out):
    def project(h):
        B, T, _ = h.shape
        q, k, v = jnp.split(h @ w_in, 3, axis=-1)
        shp = (B, T, C_HEADS, C_HEAD_DIM)
        return rms_norm(q.reshape(shp), q_g), rms_norm(k.reshape(shp), k_g), v.reshape(shp)

    qp, kp, vp = project(hp)
    yp = band_attn_prompt(qp, kp, vp, table) @ w_out
    qs, ks, vs = project(hs)
    Bs, Ts = hs.shape[:2]
    W = cache_k.shape[1]
    key_pos = jnp.concatenate([jnp.arange(W) - W, jnp.arange(Ts)])
    bias = rel_bias(table, jnp.arange(Ts)[:, None] - key_pos[None, :])
    o_s = band_core(qs, jnp.concatenate([cache_k, ks], axis=1),
                    jnp.concatenate([cache_v, vs], axis=1), bias, None)
    ys = o_s.reshape(Bs, Ts, -1) @ w_out
    keep = min(C_BAND_PAST, hp.shape[1])
    return yp, ys, kp[:, -keep:], vp[:, -keep:], ks, vs


def sq_relu_mlp(h, w1, w2):
    return jnp.square(jax.nn.relu(h @ w1)) @ w2


def setup_inputs(seed: int = 0) -> dict:
    key = jax.random.key(seed)
    keys = jax.random.split(key, 40)
    counter = iter(range(40))

    def nrm(shape, scale):
        return jax.random.normal(keys[next(counter)], shape, jnp.float32) * scale

    def gain(shape):
        return 1.0 + nrm(shape, 0.02)

    c_win = min(C_BAND_PAST, PAST_LEN)
    d = D_MODEL
    lam_u = jax.random.uniform(keys[next(counter)], (N_B, B_WIDTH), jnp.float32, 0.9, 0.999)
    sig = lam_u ** (1.0 / B_C)
    b_lambda = jnp.log(sig) - jnp.log1p(-sig)
    return {
        "x_prompt": nrm((BATCH, SEQ, d), 1.0),
        "x_sample": nrm((DEC_BATCH, DEC_SEQ, d), 1.0),
        "cache_a_k": nrm((N_A, DEC_BATCH, PAST_LEN, A_HEADS, 2, A_HEAD_DIM), 1.0),
        "cache_a_v": nrm((N_A, DEC_BATCH, PAST_LEN, A_HEADS, 2 * A_HEAD_DIM), 1.0),
        "state_b_conv": nrm((N_B, DEC_BATCH, B_CONV_W - 1, B_WIDTH), 1.0),
        "state_b_h": nrm((N_B, DEC_BATCH, B_WIDTH), 0.5),
        "cache_c_k": nrm((N_C, DEC_BATCH, c_win, C_HEADS, C_HEAD_DIM), 1.0),
        "cache_c_v": nrm((N_C, DEC_BATCH, c_win, C_HEADS, C_HEAD_DIM), 1.0),
        "norm_mix_g": gain((DEPTH, d)),
        "norm_mlp_g": gain((DEPTH, d)),
        "norm_final_g": gain((d,)),
        "a_w_in": nrm((N_A, d, 3 * d), d ** -0.5),
        "a_q_norm_g": gain((N_A, A_HEAD_DIM)),
        "a_k_norm_g": gain((N_A, A_HEAD_DIM)),
        "a_lambda": nrm((N_A, 4, A_HEAD_DIM), 0.1),
        "a_subln_g": gain((N_A, 2 * A_HEAD_DIM)),
        "a_w_out": nrm((N_A, d, d), d ** -0.5),
        "b_w_in": nrm((N_B, d, 2 * B_WIDTH), d ** -0.5),
        "b_b_in": nrm((N_B, 2 * B_WIDTH), 0.02),
        "b_conv_w": nrm((N_B, B_CONV_W, B_WIDTH), B_CONV_W ** -0.5),
        "b_conv_b": nrm((N_B, B_WIDTH), 0.02),
        "b_gate_a_w": nrm((N_B, B_BLOCKS, B_BLOCK_W, B_BLOCK_W), B_BLOCK_W ** -0.5),
        "b_gate_a_b": nrm((N_B, B_WIDTH), 0.02),
        "b_gate_x_w": nrm((N_B, B_BLOCKS, B_BLOCK_W, B_BLOCK_W), B_BLOCK_W ** -0.5),
        "b_gate_x_b": nrm((N_B, B_WIDTH), 0.02),
        "b_lambda": b_lambda,
        "b_w_out": nrm((N_B, B_WIDTH, d), B_WIDTH ** -0.5),
        "c_w_in": nrm((N_C, d, 3 * d), d ** -0.5),
        "c_q_norm_g": gain((N_C, C_HEAD_DIM)),
        "c_k_norm_g": gain((N_C, C_HEAD_DIM)),
        "c_rel_bias": nrm((N_C, 2 * C_REL_CLIP + 1, C_HEADS), 0.2),
        "c_w_out": nrm((N_C, d, d), d ** -0.5),
        "mlp_w1": nrm((DEPTH, d, D_FF), d ** -0.5),
        "mlp_w2": nrm((DEPTH, D_FF, d), 0.5 * D_FF ** -0.5),
    }


def reference(x_prompt, x_sample, cache_a_k, cache_a_v, state_b_conv, state_b_h, cache_c_k, cache_c_v,
              norm_mix_g, norm_mlp_g, norm_final_g,
              a_w_in, a_q_norm_g, a_k_norm_g, a_lambda, a_subln_g, a_w_out,
              b_w_in, b_b_in, b_conv_w, b_conv_b, b_gate_a_w, b_gate_a_b, b_gate_x_w, b_gate_x_b,
              b_lambda, b_w_out,
              c_w_in, c_q_norm_g, c_k_norm_g, c_rel_bias, c_w_out,
              mlp_w1, mlp_w2):
    xp, xs = x_prompt, x_sample
    a_kp, a_vp, a_ks, a_vs = [], [], [], []
    b_cp, b_hp, b_cs, b_hs = [], [], [], []
    c_kp, c_vp, c_ks, c_vs = [], [], [], []
    for layer in range(DEPTH):
        kind = layer % N_MIXERS
        idx = layer // N_MIXERS
        hp = rms_norm(xp, norm_mix_g[layer])
        hs = rms_norm(xs, norm_mix_g[layer])
        if kind == 0:
            lam_init = 0.8 - 0.6 * math.exp(-0.3 * layer)
            yp, ys, kp, vp, ks, vs = diff_mixer(
                hp, hs, cache_a_k[idx], cache_a_v[idx], a_w_in[idx], a_q_norm_g[idx], a_k_norm_g[idx],
                a_lambda[idx], a_subln_g[idx], a_w_out[idx], lam_init)
            a_kp.append(kp); a_vp.append(vp); a_ks.append(ks); a_vs.append(vs)
        elif kind == 1:
            blk = (b_w_in[idx], b_b_in[idx], b_conv_w[idx], b_conv_b[idx], b_gate_a_w[idx], b_gate_a_b[idx],
                   b_gate_x_w[idx], b_gate_x_b[idx], b_lambda[idx], b_w_out[idx])
            Bp = hp.shape[0]
            yp, cp, h_p = rglru_block(hp, jnp.zeros((Bp, B_CONV_W - 1, B_WIDTH), hp.dtype),
                                      jnp.zeros((Bp, B_WIDTH), state_b_h.dtype), *blk)
            ys, cs, h_s = rglru_block(hs, state_b_conv[idx], state_b_h[idx], *blk)
            b_cp.append(cp); b_hp.append(h_p); b_cs.append(cs); b_hs.append(h_s)
        else:
            yp, ys, kp, vp, ks, vs = band_mixer(
                hp, hs, cache_c_k[idx], cache_c_v[idx], c_w_in[idx], c_q_norm_g[idx], c_k_norm_g[idx],
                c_rel_bias[idx], c_w_out[idx])
            c_kp.append(kp); c_vp.append(vp); c_ks.append(ks); c_vs.append(vs)
        xp = xp + yp
        xs = xs + ys
        xp = xp + sq_relu_mlp(rms_norm(xp, norm_mlp_g[layer]), mlp_w1[layer], mlp_w2[layer])
        xs = xs + sq_relu_mlp(rms_norm(xs, norm_mlp_g[layer]), mlp_w1[layer], mlp_w2[layer])
    y_prompt = rms_norm(xp, norm_final_g)
    y_sample = rms_norm(xs, norm_final_g)
    return (y_prompt, y_sample,
            jnp.stack(a_kp), jnp.stack(a_vp), jnp.stack(a_ks), jnp.stack(a_vs),
            jnp.stack(b_cp), jnp.stack(b_hp), jnp.stack(b_cs), jnp.stack(b_hs),
            jnp.stack(c_kp), jnp.stack(c_vp), jnp.stack(c_ks), jnp.stack(c_vs))
```

```python
import functools
import math

import jax
import jax.numpy as jnp
from jax import lax
from jax.experimental import pallas as pl
from jax.experimental.pallas import tpu as pltpu

F32 = jnp.float32
BF16 = jnp.bfloat16

EPS = 1e-6
CHUNK = 64
GROUP = 64
LANES = 128
SUBLANES = 8
ROPE_THETA = 10000.0
A_SCALE = GROUP ** -0.5
B_C = 8.0
B_CONV_W = 4
B_BLOCKS = 4
C_LEFT_CHUNKS = 8
C_BAND_PAST = C_LEFT_CHUNKS * CHUNK
C_REL_CLIP = 128
NEG = -1e30
VMEM_LIMIT_BYTES = 56 * 1024 * 1024

ROW_TILE = 512
ATT_TILE = 256
DEC_KEY_TILE = 512
FF_CHUNK = 512


def _row_tile(rows, target=ROW_TILE):
    t = min(rows, target)
    assert rows % t == 0, (rows, t)
    return t


def _params(*semantics):
    return pltpu.CompilerParams(dimension_semantics=semantics, vmem_limit_bytes=VMEM_LIMIT_BYTES)


def _const_spec(shape):
    nd = len(shape)
    return pl.BlockSpec(shape, lambda *_: (0,) * nd, pipeline_mode=pl.Buffered(1))


def _rms(x, g):
    ms = jnp.mean(x * x, axis=-1, keepdims=True)
    return x * lax.rsqrt(ms + EPS) * g


def _proj_kernel(*refs, rope):
    if rope:
        (x_ref, g_ref, w_ref, qg_ref, kg_ref, sel_ref, exp_ref, cos_ref, sin_ref,
         q_ref, k_ref, v_ref, kf_ref, vf_ref) = refs
    else:
        (x_ref, g_ref, w_ref, qg_ref, kg_ref, sel_ref, exp_ref,
         q_ref, k_ref, v_ref, kf_ref, vf_ref) = refs
    x = x_ref[...]
    tm, d = x.shape
    xn = _rms(x, g_ref[...]).astype(BF16)
    qkv = jnp.dot(xn, w_ref[...], preferred_element_type=F32)
    sel = sel_ref[...]
    expand = exp_ref[...]

    def group_norm(t, gain):
        ss = jnp.dot((t * t).astype(BF16), sel, preferred_element_type=F32)
        inv = lax.rsqrt(ss * (1.0 / GROUP) + EPS)
        hi = inv.astype(BF16)
        lo = (inv - hi.astype(F32)).astype(BF16)
        full = (jnp.dot(hi, expand, preferred_element_type=F32)
                + jnp.dot(lo, expand, preferred_element_type=F32))
        return t * full * gain

    def rotary(t):
        cos = cos_ref[...]
        sin = sin_ref[...]
        lane = lax.broadcasted_iota(jnp.int32, (tm, LANES), 1)
        first_half = (lane % GROUP) < (GROUP // 2)
        outs = []
        for j in range(d // LANES):
            tj = t[:, j * LANES:(j + 1) * LANES]
            partner = jnp.where(first_half, pltpu.roll(tj, LANES - GROUP // 2, 1),
                                pltpu.roll(tj, GROUP // 2, 1))
            outs.append(tj * cos + partner * sin)
        return jnp.concatenate(outs, axis=1)

    q = group_norm(qkv[:, :d], qg_ref[...])
    k = group_norm(qkv[:, d:2 * d], kg_ref[...])
    v = qkv[:, 2 * d:]
    if rope:
        q = rotary(q)
        k = rotary(k)
    q_ref[...] = (q * A_SCALE).astype(BF16)
    k_ref[...] = k.astype(BF16)
    v_ref[...] = v.astype(BF16)
    kf_ref[...] = k
    vf_ref[...] = v


def _project(x, g, w_in, q_gain, k_gain, sel, expand, *, seq, keep, rope_tables=None):
    rows, d = x.shape
    if keep == seq:
        tm = _row_tile(rows if rope_tables is None else math.gcd(rows, rope_tables[0].shape[0]))
        n_t = keep_t = 1
    else:
        tm = _row_tile(math.gcd(seq, keep))
        n_t, keep_t = seq // tm, keep // tm
    n_steps = rows // tm

    def kept_block(i):
        return (i // n_t) * keep_t + jnp.maximum(i % n_t - (n_t - keep_t), 0)

    row_spec = pl.BlockSpec((tm, d), lambda i: (i, 0))
    in_specs = [row_spec, _const_spec((1, d)), _const_spec((d, 3 * d)), _const_spec((1, d)),
                _const_spec((1, d)), _const_spec((d, LANES)), _const_spec((LANES, d))]
    args = [x, g, w_in, q_gain, k_gain, sel, expand]
    if rope_tables is not None:
        cos, sin = rope_tables
        n_rt = cos.shape[0] // tm
        in_specs += [pl.BlockSpec((tm, LANES), lambda i: (i % n_rt, 0))] * 2
        args += [cos, sin]
    kept_rows = (rows // seq) * keep
    kept_spec = pl.BlockSpec((tm, d), lambda i: (kept_block(i), 0))
    return pl.pallas_call(
        functools.partial(_proj_kernel, rope=rope_tables is not None),
        grid=(n_steps,),
        in_specs=in_specs,
        out_specs=[row_spec, row_spec, row_spec, kept_spec, kept_spec],
        out_shape=[jax.ShapeDtypeStruct((rows, d), BF16)] * 3
        + [jax.ShapeDtypeStruct((kept_rows, d), F32)] * 2,
        compiler_params=_params("arbitrary"),
        name="qkv_project",
    )(*args)


def _stack_groups(qh):
    lane = lax.broadcasted_iota(jnp.int32, qh.shape, 1)
    zero = jnp.zeros_like(qh)
    low = lane < GROUP
    return jnp.concatenate([jnp.where(low, qh, zero), jnp.where(low, zero, qh)], axis=0)


def _scores(qs, kj):
    return lax.dot_general(qs, kj, (((1,), (1,)), ((), ())), preferred_element_type=F32)


def _online_step(carry, s, vj):
    m, l, acc = carry
    m_new = jnp.maximum(m, jnp.max(s, axis=1, keepdims=True))
    alpha = jnp.exp(m - m_new)
    p = jnp.exp(s - m_new)
    l = alpha * l + jnp.sum(p, axis=1, keepdims=True)
    acc = alpha * acc + jnp.dot(p.astype(BF16), vj, preferred_element_type=F32)
    return m_new, l, acc


def _lambda_value(lp, lam_init):
    return (jnp.exp(jnp.sum(lp[0:1] * lp[1:2], keepdims=True))
            - jnp.exp(jnp.sum(lp[2:3] * lp[3:4], keepdims=True)) + lam_init)


def _diff_combine(on, t, lam, sub_gain, lam_init):
    o = on[:t] - lam * on[t:]
    return _rms(o, sub_gain) * (1.0 - lam_init)


def _diff_prompt_kernel(q_ref, k_ref, v_ref, x_ref, wo_ref, sg_ref, lp_ref, o_ref, obuf, *, lam_init):
    i = pl.program_id(1)
    t, d = q_ref.shape
    lam = _lambda_value(lp_ref[...], lam_init)
    row = lax.broadcasted_iota(jnp.int32, (2 * t, t), 0)
    col = lax.broadcasted_iota(jnp.int32, (2 * t, t), 1)
    visible = (col // CHUNK) <= ((row % t) // CHUNK)
    for h in range(d // LANES):
        hs = slice(h * LANES, (h + 1) * LANES)
        qs = _stack_groups(q_ref[:, hs])

        def block(j, carry, diagonal):
            rows = pl.ds(pl.multiple_of(j * t, t), t)
            s = _scores(qs, k_ref[rows, hs])
            if diagonal:
                s = jnp.where(visible, s, NEG)
            return _online_step(carry, s, v_ref[rows, hs])

        init = (jnp.full((2 * t, 1), NEG, F32), jnp.zeros((2 * t, 1), F32), jnp.zeros((2 * t, LANES), F32))
        carry = lax.fori_loop(0, i, lambda j, c: block(j, c, False), init)
        _, l, acc = block(i, carry, True)
        obuf[:, hs] = _diff_combine(acc / l, t, lam, sg_ref[:, hs], lam_init).astype(BF16)
    o_ref[...] = x_ref[...] + jnp.dot(obuf[...], wo_ref[...], preferred_element_type=F32)


def _diff_attention_prompt(q, k, v, x, w_out, sub_gain, lam_p, *, batch, seq, lam_init):
    rows, d = x.shape
    t = min(ATT_TILE, seq)
    nq = seq // t
    q_spec = pl.BlockSpec((t, d), lambda b, i: (b * nq + i, 0))
    kv_spec = pl.BlockSpec((seq, d), lambda b, i: (b, 0))
    return pl.pallas_call(
        functools.partial(_diff_prompt_kernel, lam_init=lam_init),
        grid=(batch, nq),
        in_specs=[q_spec, kv_spec, kv_spec, q_spec, _const_spec((d, d)), _const_spec((1, d)),
                  _const_spec(lam_p.shape)],
        out_specs=q_spec,
        out_shape=jax.ShapeDtypeStruct((rows, d), F32),
        scratch_shapes=[pltpu.VMEM((t, d), BF16)],
        compiler_params=_params("arbitrary", "arbitrary"),
        name="diff_attention_prompt",
    )(q, k, v, x, w_out, sub_gain, lam_p)


def _diff_decode_kernel(q_ref, kc_ref, vc_ref, kn_ref, vn_ref, x_ref, wo_ref, sg_ref, lp_ref, o_ref,
                        m_scr, l_scr, acc_scr, obuf, *, lam_init):
    j = pl.program_id(1)
    t, d = q_ref.shape
    n_heads = d // LANES

    @pl.when(j == 0)
    def _():
        m_scr[...] = jnp.full(m_scr.shape, NEG, F32)
        l_scr[...] = jnp.zeros(l_scr.shape, F32)
        acc_scr[...] = jnp.zeros(acc_scr.shape, F32)

    def update(h, kh, vh):
        hs = slice(h * LANES, (h + 1) * LANES)
        s = _scores(_stack_groups(q_ref[:, hs]), kh)
        m_new, l, acc = _online_step((m_scr[h], l_scr[h], acc_scr[h]), s, vh)
        m_scr[h] = m_new
        l_scr[h] = l
        acc_scr[h] = acc

    for h in range(n_heads):
        hs = slice(h * LANES, (h + 1) * LANES)
        update(h, kc_ref[:, hs].astype(BF16), vc_ref[:, hs].astype(BF16))

    @pl.when(j == pl.num_programs(1) - 1)
    def _():
        lam = _lambda_value(lp_ref[...], lam_init)
        for h in range(n_heads):
            hs = slice(h * LANES, (h + 1) * LANES)
            update(h, kn_ref[:, hs], vn_ref[:, hs])
            on = acc_scr[h] / l_scr[h]
            obuf[:, hs] = _diff_combine(on, t, lam, sg_ref[:, hs], lam_init).astype(BF16)
        o_ref[...] = x_ref[...] + jnp.dot(obuf[...], wo_ref[...], preferred_element_type=F32)


def _diff_attention_decode(q, cache_k, cache_v, k_new, v_new, x, w_out, sub_gain, lam_p, *, lam_init):
    batch, past, d = cache_k.shape
    rows = x.shape[0]
    t = rows // batch
    tk = min(DEC_KEY_TILE, past)
    row_spec = pl.BlockSpec((t, d), lambda b, j: (b, 0))
    cache_spec = pl.BlockSpec((None, tk, d), lambda b, j: (b, j, 0))
    n_heads = d // LANES
    return pl.pallas_call(
        functools.partial(_diff_decode_kernel, lam_init=lam_init),
        grid=(batch, past // tk),
        in_specs=[row_spec, cache_spec, cache_spec, row_spec, row_spec, row_spec, _const_spec((d, d)),
                  _const_spec((1, d)), _const_spec(lam_p.shape)],
        out_specs=row_spec,
        out_shape=jax.ShapeDtypeStruct((rows, d), F32),
        scratch_shapes=[pltpu.VMEM((n_heads, 2 * t, 1), F32), pltpu.VMEM((n_heads, 2 * t, 1), F32),
                        pltpu.VMEM((n_heads, 2 * t, LANES), F32), pltpu.VMEM((t, d), BF16)],
        compiler_params=_params("arbitrary", "arbitrary"),
        name="diff_attention_decode",
    )(q, cache_k, cache_v, k_new, v_new, x, w_out, sub_gain, lam_p)


def _gather_rows(tbl, idx_of_col, width):
    n_pad = tbl.shape[1]
    t_idx = lax.broadcasted_iota(jnp.int32, (n_pad, width), 0)
    v_idx = lax.broadcasted_iota(jnp.int32, (n_pad, width), 1)
    onehot = jnp.where(t_idx == idx_of_col(v_idx), 1.0, 0.0).astype(BF16)
    hi = tbl.astype(BF16)
    r1 = tbl - hi.astype(F32)
    mid = r1.astype(BF16)
    lo = (r1 - mid.astype(F32)).astype(BF16)
    dot = lambda a: jnp.dot(a, onehot, preferred_element_type=F32)
    return dot(hi) + dot(mid) + dot(lo)


def _rel_index(rel):
    return jnp.clip(rel, -C_REL_CLIP, C_REL_CLIP) + C_REL_CLIP


def _toeplitz(vec, rows, cols, origin):
    width = vec.shape[1]
    xb = jnp.broadcast_to(vec, (rows, width))
    return pltpu.roll(xb, (width - origin) % width, 1, stride=1, stride_axis=0)[:, :cols]


def _bias_kernel(tbl_ref, bp_ref, bs_ref, *, t, n_blocks, ts, win):
    tbl = tbl_ref[...]
    heads = tbl.shape[0]
    row = lax.broadcasted_iota(jnp.int32, (t, t), 0)
    col = lax.broadcasted_iota(jnp.int32, (t, t), 1)
    for jb in range(n_blocks):
        off = (n_blocks - 1 - jb) * t
        g = _gather_rows(tbl, lambda v: _rel_index(t - 1 - v + off), 2 * t)
        key_chunk = (jb * t + col) // CHUNK - (n_blocks - 1) * t // CHUNK
        q_chunk = row // CHUNK
        visible = (key_chunk <= q_chunk) & (key_chunk >= q_chunk - C_LEFT_CHUNKS)
        for h in range(heads):
            tile = _toeplitz(g[h:h + 1, :], t, t, t - 1)
            bp_ref[jb, h // 2, (h % 2) * t:(h % 2 + 1) * t, :] = jnp.where(visible, tile, NEG)
    width = bs_ref.shape[-1]
    wide = 1 << (width + ts).bit_length()
    g = _gather_rows(tbl, lambda v: _rel_index(win + ts - 1 - v), wide)
    for h in range(heads):
        tile = _toeplitz(g[h:h + 1, :], ts, width, ts - 1)
        bs_ref[h // 2, (h % 2) * ts:(h % 2 + 1) * ts, :] = tile


def _band_bias(table, *, t, n_blocks, ts, win):
    n, heads = table.shape
    n_pad = -(-n // LANES) * LANES
    tbl = jnp.pad(table.T, ((0, 0), (0, n_pad - n)))
    return pl.pallas_call(
        functools.partial(_bias_kernel, t=t, n_blocks=n_blocks, ts=ts, win=win),
        out_shape=[jax.ShapeDtypeStruct((n_blocks, heads // 2, 2 * t, t), F32),
                   jax.ShapeDtypeStruct((heads // 2, 2 * ts, win + ts), F32)],
        compiler_params=pltpu.CompilerParams(vmem_limit_bytes=VMEM_LIMIT_BYTES),
        name="band_bias",
    )(tbl)


def _band_prompt_kernel(q_ref, k_ref, v_ref, b_ref, x_ref, wo_ref, o_ref, obuf):
    i = pl.program_id(1)
    t, d = q_ref.shape
    n_blocks = b_ref.shape[0]
    first = jnp.maximum(n_blocks - 1 - i, 0)
    lane = lax.broadcasted_iota(jnp.int32, (t, LANES), 1)
    for h in range(d // LANES):
        hs = slice(h * LANES, (h + 1) * LANES)
        qs = _stack_groups(q_ref[:, hs])

        def block(jb, carry):
            rows = pl.ds(pl.multiple_of((i - (n_blocks - 1) + jb) * t, t), t)
            s = _scores(qs, k_ref[rows, hs]) + b_ref[jb, h]
            return _online_step(carry, s, v_ref[rows, hs])

        init = (jnp.full((2 * t, 1), NEG, F32), jnp.zeros((2 * t, 1), F32), jnp.zeros((2 * t, LANES), F32))
        _, l, acc = lax.fori_loop(first, n_blocks, block, init)
        on = acc / l
        obuf[:, hs] = jnp.where(lane < GROUP, on[:t], on[t:]).astype(BF16)
    o_ref[...] = x_ref[...] + jnp.dot(obuf[...], wo_ref[...], preferred_element_type=F32)


def _band_attention_prompt(q, k, v, bias, x, w_out, *, batch, seq):
    rows, d = x.shape
    t = bias.shape[-1]
    nq = seq // t
    q_spec = pl.BlockSpec((t, d), lambda b, i: (b * nq + i, 0))
    kv_spec = pl.BlockSpec((seq, d), lambda b, i: (b, 0))
    return pl.pallas_call(
        _band_prompt_kernel,
        grid=(batch, nq),
        in_specs=[q_spec, kv_spec, kv_spec, _const_spec(bias.shape), q_spec, _const_spec((d, d))],
        out_specs=q_spec,
        out_shape=jax.ShapeDtypeStruct((rows, d), F32),
        scratch_shapes=[pltpu.VMEM((t, d), BF16)],
        compiler_params=_params("arbitrary", "arbitrary"),
        name="band_attention_prompt",
    )(q, k, v, bias, x, w_out)


def _band_decode_kernel(q_ref, kc_ref, vc_ref, kn_ref, vn_ref, b_ref, x_ref, wo_ref, o_ref, obuf):
    t, d = q_ref.shape
    win = kc_ref.shape[0]
    lane = lax.broadcasted_iota(jnp.int32, (t, LANES), 1)
    for h in range(d // LANES):
        hs = slice(h * LANES, (h + 1) * LANES)
        qs = _stack_groups(q_ref[:, hs])
        bias = b_ref[h]
        s_c = _scores(qs, kc_ref[:, hs].astype(BF16)) + bias[:, :win]
        s_n = _scores(qs, kn_ref[:, hs]) + bias[:, win:]
        m = jnp.maximum(jnp.max(s_c, axis=1, keepdims=True), jnp.max(s_n, axis=1, keepdims=True))
        p_c = jnp.exp(s_c - m)
        p_n = jnp.exp(s_n - m)
        l = jnp.sum(p_c, axis=1, keepdims=True) + jnp.sum(p_n, axis=1, keepdims=True)
        acc = (jnp.dot(p_c.astype(BF16), vc_ref[:, hs].astype(BF16), preferred_element_type=F32)
               + jnp.dot(p_n.astype(BF16), vn_ref[:, hs], preferred_element_type=F32))
        on = acc / l
        obuf[:, hs] = jnp.where(lane < GROUP, on[:t], on[t:]).astype(BF16)
    o_ref[...] = x_ref[...] + jnp.dot(obuf[...], wo_ref[...], preferred_element_type=F32)


def _band_attention_decode(q, cache_k, cache_v, k_new, v_new, bias, x, w_out):
    batch, win, d = cache_k.shape
    rows = x.shape[0]
    t = rows // batch
    row_spec = pl.BlockSpec((t, d), lambda b: (b, 0))
    cache_spec = pl.BlockSpec((None, win, d), lambda b: (b, 0, 0))
    return pl.pallas_call(
        _band_decode_kernel,
        grid=(batch,),
        in_specs=[row_spec, cache_spec, cache_spec, row_spec, row_spec, _const_spec(bias.shape), row_spec,
                  _const_spec((d, d))],
        out_specs=row_spec,
        out_shape=jax.ShapeDtypeStruct((rows, d), F32),
        scratch_shapes=[pltpu.VMEM((t, d), BF16)],
        compiler_params=_params("arbitrary"),
        name="band_attention_decode",
    )(q, cache_k, cache_v, k_new, v_new, bias, x, w_out)


def _rglru_kernel(x_ref, g_ref, win_ref, bin_ref, cw_ref, cb_ref, gaw_ref, gab_ref, gxw_ref, gxb_ref,
                  lam_ref, wout_ref, hist_ref, h0_ref, o_ref, cst_ref, hst_ref, ubuf, hcar, a_scr, b_scr):
    tm, w = a_scr.shape[0] - SUBLANES, a_scr.shape[1]
    pad = SUBLANES

    @pl.when(pl.program_id(1) == 0)
    def _():
        ubuf[0:pad, :] = hist_ref[...]
        hcar[...] = h0_ref[...]
        a_scr[0:pad, :] = jnp.zeros((pad, w), F32)
        b_scr[0:pad, :] = jnp.zeros((pad, w), F32)

    x = x_ref[...]
    xn = _rms(x, g_ref[...]).astype(BF16)
    gu = jnp.dot(xn, win_ref[...], preferred_element_type=F32) + bin_ref[...]
    gate = jax.nn.gelu(gu[:, :w])
    u = gu[:, w:]
    ubuf[pad:pad + tm, :] = u
    cw = cw_ref[...]
    xc = cb_ref[...] + cw[3:4] * u
    for j in range(B_CONV_W - 1):
        xc = xc + cw[j:j + 1] * ubuf[pad - 3 + j:pad - 3 + j + tm, :]

    xcb = xc.astype(BF16)
    bw = w // B_BLOCKS
    r_parts, i_parts = [], []
    for n in range(B_BLOCKS):
        blk = xcb[:, n * bw:(n + 1) * bw]
        r_parts.append(jnp.dot(blk, gaw_ref[n], preferred_element_type=F32))
        i_parts.append(jnp.dot(blk, gxw_ref[n], preferred_element_type=F32))
    r = jax.nn.sigmoid(jnp.concatenate(r_parts, axis=1) + gab_ref[...])
    ig = jax.nn.sigmoid(jnp.concatenate(i_parts, axis=1) + gxb_ref[...])
    z = -lam_ref[...]
    softplus = jnp.maximum(z, 0.0) + jnp.log1p(jnp.exp(-jnp.abs(z)))
    a = jnp.exp(-B_C * r * softplus)
    b = jnp.sqrt(1.0 - a * a) * (ig * xc)

    sub = lax.broadcasted_iota(jnp.int32, (tm, w), 0) % SUBLANES
    for s in (1, 2, 4):
        a_scr[pad:pad + tm, :] = a
        b_scr[pad:pad + tm, :] = b
        a_prev = a_scr[pad - s:pad - s + tm, :]
        b_prev = b_scr[pad - s:pad - s + tm, :]
        inside = sub >= s
        b = jnp.where(inside, a * b_prev + b, b)
        a = jnp.where(inside, a * a_prev, a)
    h_prev = hcar[...]
    for gidx in range(tm // SUBLANES):
        rows = slice(gidx * SUBLANES, (gidx + 1) * SUBLANES)
        hg = a[rows] * h_prev + b[rows]
        b_scr[pad + gidx * SUBLANES:pad + (gidx + 1) * SUBLANES, :] = hg
        h_prev = jnp.broadcast_to(hg[SUBLANES - 1:SUBLANES, :], (SUBLANES, w))
    hcar[...] = h_prev
    hs = b_scr[pad:pad + tm, :]

    y = jnp.dot((hs * gate).astype(BF16), wout_ref[...], preferred_element_type=F32)
    o_ref[...] = x + y
    ubuf[0:pad, :] = u[tm - pad:tm, :]
    cst_ref[...] = u[tm - pad:tm, :]
    hst_ref[...] = h_prev


def _rglru(x, g, p, hist, h0, *, batch, seq):
    rows, d = x.shape
    w = p["w_out"].shape[0]
    tm = _row_tile(seq)
    nt = seq // tm
    row_spec = pl.BlockSpec((tm, d), lambda b, t: (b * nt + t, 0))
    state_spec = pl.BlockSpec((None, SUBLANES, w), lambda b, t: (b, 0, 0))
    consts = [g, p["w_in"], p["b_in"], p["conv_w"], p["conv_b"], p["ga_w"], p["ga_b"], p["gx_w"], p["gx_b"],
              p["lam"], p["w_out"]]
    return pl.pallas_call(
        _rglru_kernel,
        grid=(batch, nt),
        in_specs=[row_spec] + [_const_spec(c.shape) for c in consts] + [state_spec, state_spec],
        out_specs=[row_spec, state_spec, state_spec],
        out_shape=[jax.ShapeDtypeStruct((rows, d), F32), jax.ShapeDtypeStruct((batch, SUBLANES, w), F32),
                   jax.ShapeDtypeStruct((batch, SUBLANES, w), F32)],
        scratch_shapes=[pltpu.VMEM((tm + SUBLANES, w), F32), pltpu.VMEM((SUBLANES, w), F32),
                        pltpu.VMEM((tm + SUBLANES, w), F32), pltpu.VMEM((tm + SUBLANES, w), F32)],
        compiler_params=_params("arbitrary", "arbitrary"),
        name="rglru_block",
    )(x, *consts, hist, h0)


def _mlp_kernel(x_ref, g_ref, w1_ref, w2_ref, gf_ref, o_ref, *, final_norm):
    x = x_ref[...]
    xn = _rms(x, g_ref[...]).astype(BF16)
    acc = x
    for c in range(w1_ref.shape[1] // FF_CHUNK):
        cols = slice(c * FF_CHUNK, (c + 1) * FF_CHUNK)
        h = jnp.dot(xn, w1_ref[:, cols], preferred_element_type=F32)
        h = jnp.square(jnp.maximum(h, 0.0)).astype(BF16)
        acc = acc + jnp.dot(h, w2_ref[cols, :], preferred_element_type=F32)
    if final_norm:
        acc = _rms(acc, gf_ref[...])
    o_ref[...] = acc


def _mlp(x, g, w1, w2, g_final, *, final_norm):
    rows, d = x.shape
    tm = _row_tile(rows)
    row_spec = pl.BlockSpec((tm, d), lambda i: (i, 0))
    return pl.pallas_call(
        functools.partial(_mlp_kernel, final_norm=final_norm),
        grid=(rows // tm,),
        in_specs=[row_spec, _const_spec((1, d)), _const_spec(w1.shape), _const_spec(w2.shape),
                  _const_spec((1, d))],
        out_specs=row_spec,
        out_shape=jax.ShapeDtypeStruct((rows, d), F32),
        compiler_params=_params("arbitrary"),
        name="mlp",
    )(x, g, w1, w2, g_final)


def _rope_tables(pos):
    half = GROUP // 2
    inv = ROPE_THETA ** (-jnp.arange(half, dtype=F32) / half)
    ang = pos.astype(F32)[:, None] * inv[None, :]
    cos = jnp.tile(jnp.cos(ang), (1, LANES // half))
    sin = jnp.tile(jnp.concatenate([-jnp.sin(ang), jnp.sin(ang)], axis=1), (1, LANES // GROUP))
    return cos, sin


def _group_selectors(d):
    grp = jnp.arange(d) // GROUP
    sel = (grp[:, None] == jnp.arange(LANES)[None, :]).astype(BF16)
    return sel, sel.T


def kernel(x_prompt, x_sample, cache_a_k, cache_a_v, state_b_conv, state_b_h, cache_c_k, cache_c_v, norm_mix_g, norm_mlp_g, norm_final_g, a_w_in, a_q_norm_g, a_k_norm_g, a_lambda, a_subln_g, a_w_out, b_w_in, b_b_in, b_conv_w, b_conv_b, b_gate_a_w, b_gate_a_b, b_gate_x_w, b_gate_x_b, b_lambda, b_w_out, c_w_in, c_q_norm_g, c_k_norm_g, c_rel_bias, c_w_out, mlp_w1, mlp_w2):
    batch, seq, d = x_prompt.shape
    dbatch, dseq, _ = x_sample.shape
    depth = norm_mix_g.shape[0]
    past = cache_a_k.shape[2]
    xp = x_prompt.reshape(batch * seq, d)
    xs = x_sample.reshape(dbatch * dseq, d)
    row = lambda v: v.reshape(1, -1).astype(F32)
    tile_gain = lambda v: jnp.tile(v.astype(F32), d // v.shape[0]).reshape(1, d)
    sel, expand = _group_selectors(d)
    rope_p = _rope_tables(jnp.arange(seq))
    rope_s = tuple(jnp.tile(t, (dbatch, 1)) for t in _rope_tables(past + jnp.arange(dseq)))
    keep = min(C_BAND_PAST, seq)
    att_t = min(ATT_TILE, seq)
    n_band_blocks = C_BAND_PAST // att_t + 1

    outs = {name: [] for name in ("akp", "avp", "aks", "avs", "bcp", "bhp", "bcs", "bhs", "ckp", "cvp", "cks", "cvs")}
    for layer in range(depth):
        kind, idx = layer % 3, layer // 3
        g_mix = row(norm_mix_g[layer])
        if kind == 0:
            lam_init = 0.8 - 0.6 * math.exp(-0.3 * layer)
            w_in = a_w_in[idx].astype(BF16)
            w_out = a_w_out[idx].astype(BF16)
            qg, kg, sg = tile_gain(a_q_norm_g[idx]), tile_gain(a_k_norm_g[idx]), tile_gain(a_subln_g[idx])
            lam_p = a_lambda[idx].astype(F32)
            q, k, v, kf, vf = _project(xp, g_mix, w_in, qg, kg, sel, expand, seq=seq, keep=seq, rope_tables=rope_p)
            xp = _diff_attention_prompt(q, k, v, xp, w_out, sg, lam_p, batch=batch, seq=seq, lam_init=lam_init)
            outs["akp"].append(kf.reshape(batch, seq, d // LANES, 2, GROUP))
            outs["avp"].append(vf.reshape(batch, seq, d // LANES, LANES))
            q, k, v, kf, vf = _project(xs, g_mix, w_in, qg, kg, sel, expand, seq=dseq, keep=dseq, rope_tables=rope_s)
            xs = _diff_attention_decode(q, cache_a_k[idx].reshape(dbatch, past, d),
                                        cache_a_v[idx].reshape(dbatch, past, d), k, v, xs, w_out, sg, lam_p,
                                        lam_init=lam_init)
            outs["aks"].append(kf.reshape(dbatch, dseq, d // LANES, 2, GROUP))
            outs["avs"].append(vf.reshape(dbatch, dseq, d // LANES, LANES))
        elif kind == 1:
            w = b_w_out.shape[1]
            p = dict(w_in=b_w_in[idx].astype(BF16), b_in=row(b_b_in[idx]), conv_w=b_conv_w[idx].astype(F32),
                     conv_b=row(b_conv_b[idx]), ga_w=b_gate_a_w[idx].astype(BF16), ga_b=row(b_gate_a_b[idx]),
                     gx_w=b_gate_x_w[idx].astype(BF16), gx_b=row(b_gate_x_b[idx]), lam=row(b_lambda[idx]),
                     w_out=b_w_out[idx].astype(BF16))
            zeros = jnp.zeros((batch, SUBLANES, w), F32)
            xp, cst, hst = _rglru(xp, g_mix, p, zeros, zeros, batch=batch, seq=seq)
            outs["bcp"].append(cst[:, SUBLANES - (B_CONV_W - 1):])
            outs["bhp"].append(hst[:, 0])
            hist = jnp.pad(state_b_conv[idx].astype(F32), ((0, 0), (SUBLANES - (B_CONV_W - 1), 0), (0, 0)))
            h0 = jnp.broadcast_to(state_b_h[idx].astype(F32)[:, None, :], (dbatch, SUBLANES, w))
            xs, cst, hst = _rglru(xs, g_mix, p, hist, h0, batch=dbatch, seq=dseq)
            outs["bcs"].append(cst[:, SUBLANES - (B_CONV_W - 1):])
            outs["bhs"].append(hst[:, 0])
        else:
            w_in = c_w_in[idx].astype(BF16)
            w_out = c_w_out[idx].astype(BF16)
            qg, kg = tile_gain(c_q_norm_g[idx]), tile_gain(c_k_norm_g[idx])
            heads = c_rel_bias.shape[2]
            win = cache_c_k.shape[2]
            bias_p, bias_s = _band_bias(c_rel_bias[idx].astype(F32), t=att_t, n_blocks=n_band_blocks, ts=dseq, win=win)
            q, k, v, kf, vf = _project(xp, g_mix, w_in, qg, kg, sel, expand, seq=seq, keep=keep)
            xp = _band_attention_prompt(q, k, v, bias_p, xp, w_out, batch=batch, seq=seq)
            outs["ckp"].append(kf.reshape(batch, keep, heads, GROUP))
            outs["cvp"].append(vf.reshape(batch, keep, heads, GROUP))
            q, k, v, kf, vf = _project(xs, g_mix, w_in, qg, kg, sel, expand, seq=dseq, keep=dseq)
            xs = _band_attention_decode(q, cache_c_k[idx].reshape(dbatch, win, d), cache_c_v[idx].reshape(dbatch, win, d),
                                        k, v, bias_s, xs, w_out)
            outs["cks"].append(kf.reshape(dbatch, dseq, heads, GROUP))
            outs["cvs"].append(vf.reshape(dbatch, dseq, heads, GROUP))
        g_mlp = row(norm_mlp_g[layer])
        w1 = mlp_w1[layer].astype(BF16)
        w2 = mlp_w2[layer].astype(BF16)
        last = layer == depth - 1
        xp = _mlp(xp, g_mlp, w1, w2, row(norm_final_g), final_norm=last)
        xs = _mlp(xs, g_mlp, w1, w2, row(norm_final_g), final_norm=last)
    st = lambda name: jnp.stack(outs[name])
    return (xp.reshape(batch, seq, d), xs.reshape(dbatch, dseq, d),
            st("akp"), st("avp"), st("aks"), st("avs"),
            st("bcp"), st("bhp"), st("bcs"), st("bhs"),
            st("ckp"), st("cvp"), st("cks"), st("cvs"))
```

```python
import functools
import math

import jax
import jax.numpy as jnp
from jax import lax
from jax.experimental import pallas as pl
from jax.experimental.pallas import tpu as pltpu

F32 = jnp.float32
BF16 = jnp.bfloat16

EPS = 1e-6
CHUNK = 64
GROUP = 64
LANES = 128
SUBLANES = 8
ROPE_THETA = 10000.0
LOG2E = 1.0 / math.log(2.0)
Q_SCALE = GROUP ** -0.5 * LOG2E
B_C = 8.0
B_CONV_W = 4
B_BLOCKS = 4
C_LEFT_CHUNKS = 8
C_BAND_PAST = C_LEFT_CHUNKS * CHUNK
C_REL_CLIP = 128
NEG = -1e30
VMEM_LIMIT_BYTES = 56 * 1024 * 1024

ROW_TILE = 512
ATT_TILE = 256
DEC_KEY_TILE = 2048
FF_CHUNK = 512


def _row_tile(rows, target=ROW_TILE):
    t = min(rows, target)
    assert rows % t == 0, (rows, t)
    return t


def _params(*semantics):
    return pltpu.CompilerParams(dimension_semantics=semantics, vmem_limit_bytes=VMEM_LIMIT_BYTES)


def _const_spec(shape):
    nd = len(shape)
    return pl.BlockSpec(shape, lambda *_: (0,) * nd, pipeline_mode=pl.Buffered(1))


def _rms(x, g):
    ms = jnp.mean(x * x, axis=-1, keepdims=True)
    return x * lax.rsqrt(ms + EPS) * g


def _head(h):
    return slice(h * LANES, (h + 1) * LANES)


def _proj_kernel(*refs, rope, prompt, kf_layout, vf_layout, n_prev, n_t, keep_t, att_t):
    it = iter(refs)
    x_ref, g_ref, w_ref, qg_ref, kg_ref, bd_ref = (next(it) for _ in range(6))
    cos_ref, sin_ref = (next(it), next(it)) if rope else (None, None)
    pk_ref, pv_ref = (next(it), next(it)) if n_prev else (None, None)
    q_ref, k_ref, v_ref, kf_ref, vf_ref = (next(it) for _ in range(5))
    x = x_ref[...]
    tm, d = x.shape
    xn = _rms(x, g_ref[...]).astype(BF16)
    qkv = jnp.dot(xn, w_ref[...], preferred_element_type=F32)
    bd = bd_ref[...]

    def group_norm(t, gain):
        sq = (t * t).astype(BF16)
        ss = jnp.concatenate([jnp.dot(sq[:, _head(j)], bd, preferred_element_type=F32)
                              for j in range(d // LANES)], axis=1)
        return t * lax.rsqrt(ss * (1.0 / GROUP) + EPS) * gain

    def rotary(t):
        cos = cos_ref[...]
        sin = sin_ref[...]
        lane = lax.broadcasted_iota(jnp.int32, (tm, LANES), 1)
        first_half = (lane % GROUP) < (GROUP // 2)
        outs = []
        for j in range(d // LANES):
            tj = t[:, _head(j)]
            partner = jnp.where(first_half, pltpu.roll(tj, LANES - GROUP // 2, 1),
                                pltpu.roll(tj, GROUP // 2, 1))
            outs.append(tj * cos + partner * sin)
        return jnp.concatenate(outs, axis=1)

    q = group_norm(qkv[:, :d], qg_ref[...])
    k = group_norm(qkv[:, d:2 * d], kg_ref[...])
    v = qkv[:, 2 * d:]
    if rope:
        q = rotary(q)
        k = rotary(k)
    q = q * Q_SCALE
    k_ref[...] = k.astype(BF16)
    v_t = v.T if (prompt or vf_layout == "cols") else None
    if prompt:
        q_ref[...] = q.T.astype(BF16)
        for c in range(tm // att_t):
            v_ref[c] = v_t[:, c * att_t:(c + 1) * att_t].astype(BF16)
    else:
        q_ref[...] = q.astype(BF16)
        v_ref[...] = v.astype(BF16)

    def write_f32():
        if n_prev:
            kf_ref[0:n_prev] = pk_ref[...]
            vf_ref[0:n_prev] = pv_ref[...]
        kf_ref[n_prev] = k.T if kf_layout == "cols" else k
        if vf_layout == "heads":
            vf_new = vf_ref.at[n_prev]
            for h in range(d // LANES):
                vf_new[pl.ds(h, tm, stride=d // LANES), :] = v[:, _head(h)]
        else:
            vf_ref[n_prev] = v_t if vf_layout == "cols" else v

    if keep_t < n_t:
        pl.when(pl.program_id(0) % n_t >= n_t - keep_t)(write_f32)
    else:
        write_f32()


def _project(x, g, w_in, q_gain, k_gain, bd, *, batch, seq, keep, rope_tables=None, prompt=False,
             kf_layout="rows", vf_layout="rows", prev_k=None, prev_v=None):
    rows, d = x.shape
    n_heads = d // LANES
    if keep == seq and not prompt:
        tm = _row_tile(rows if rope_tables is None else math.gcd(rows, rope_tables[0].shape[0]))
        n_t = keep_t = 1
    else:
        tm = _row_tile(math.gcd(seq, keep))
        n_t, keep_t = seq // tm, keep // tm
    att_t = min(ATT_TILE, tm)
    n_prev = 0 if prev_k is None else prev_k.shape[0]
    n = n_prev + 1

    kept_tile = lambda i: jnp.maximum(i % n_t - (n_t - keep_t), 0)
    kept_row_block = lambda i: (i // n_t) * keep_t + kept_tile(i)

    def f32_spec(layout, m):
        if layout == "cols":
            return pl.BlockSpec((m, None, d, tm), lambda i: (0, i // n_t, 0, kept_tile(i)))
        if layout == "heads":
            return pl.BlockSpec((m, tm * n_heads, LANES), lambda i: (0, kept_row_block(i), 0))
        return pl.BlockSpec((m, tm, d), lambda i: (0, kept_row_block(i), 0))

    def f32_shape(layout):
        shape = {"cols": (n, batch, d, keep), "heads": (n, batch * keep * n_heads, LANES),
                 "rows": (n, batch * keep, d)}[layout]
        return jax.ShapeDtypeStruct(shape, F32)

    row_spec = pl.BlockSpec((tm, d), lambda i: (i, 0))
    in_specs = [row_spec, _const_spec((1, d)), _const_spec((d, 3 * d)), _const_spec((1, d)),
                _const_spec((1, d)), _const_spec((LANES, LANES))]
    args = [x, g, w_in, q_gain, k_gain, bd]
    if rope_tables is not None:
        cos, sin = rope_tables
        n_rt = cos.shape[0] // tm
        in_specs += [pl.BlockSpec((tm, LANES), lambda i: (i % n_rt, 0))] * 2
        args += [cos, sin]
    if n_prev:
        in_specs += [f32_spec(kf_layout, n_prev), f32_spec(vf_layout, n_prev)]
        args += [prev_k, prev_v]
    if prompt:
        q_spec = pl.BlockSpec((None, d, tm), lambda i: (i // n_t, 0, i % n_t))
        q_shape = jax.ShapeDtypeStruct((batch, d, seq), BF16)
        v_spec = pl.BlockSpec((tm // att_t, d, att_t), lambda i: (i, 0, 0))
        v_shape = jax.ShapeDtypeStruct((rows // att_t, d, att_t), BF16)
    else:
        q_spec = v_spec = row_spec
        q_shape = v_shape = jax.ShapeDtypeStruct((rows, d), BF16)
    return pl.pallas_call(
        functools.partial(_proj_kernel, rope=rope_tables is not None, prompt=prompt, kf_layout=kf_layout,
                          vf_layout=vf_layout, n_prev=n_prev, n_t=n_t, keep_t=keep_t, att_t=att_t),
        grid=(rows // tm,),
        in_specs=in_specs,
        out_specs=[q_spec, row_spec, v_spec, f32_spec(kf_layout, n), f32_spec(vf_layout, n)],
        out_shape=[q_shape, jax.ShapeDtypeStruct((rows, d), BF16), v_shape, f32_shape(kf_layout),
                   f32_shape(vf_layout)],
        compiler_params=_params("arbitrary"),
        name="qkv_project",
    )(*args)


def _lambda_value(lp, lam_init):
    return (jnp.exp(jnp.sum(lp[0:1] * lp[1:2], keepdims=True))
            - jnp.exp(jnp.sum(lp[2:3] * lp[3:4], keepdims=True)) + lam_init)


def _reset_state(m_scr, l_scr, acc_scr):
    m_scr[...] = jnp.full(m_scr.shape, NEG, F32)
    l_scr[...] = jnp.zeros(l_scr.shape, F32)
    acc_scr[...] = jnp.zeros(acc_scr.shape, F32)


def _stack_groups_t(qt):
    feat = lax.broadcasted_iota(jnp.int32, qt.shape, 0)
    zero = jnp.zeros_like(qt)
    low = feat < GROUP
    return jnp.concatenate([jnp.where(low, qt, zero), jnp.where(low, zero, qt)], axis=1)


def _update_state_t(m_scr, l_scr, acc_scr, h, s, vt):
    m = m_scr[h]
    m_new = jnp.maximum(m, jnp.max(s, axis=0, keepdims=True))
    alpha = jnp.exp2(m - m_new)
    p = jnp.exp2(s - m_new)
    l_scr[h] = alpha * l_scr[h] + jnp.sum(p, axis=0, keepdims=True)
    acc_scr[h] = alpha * acc_scr[h] + jnp.dot(vt, p.astype(BF16), preferred_element_type=F32)
    m_scr[h] = m_new


def _state_scratch_t(n_heads, t):
    return [pltpu.VMEM((n_heads, 1, 2 * t), F32), pltpu.VMEM((n_heads, 1, 2 * t), F32),
            pltpu.VMEM((n_heads, LANES, 2 * t), F32)]


def _stack_groups(qh):
    lane = lax.broadcasted_iota(jnp.int32, qh.shape, 1)
    zero = jnp.zeros_like(qh)
    low = lane < GROUP
    return jnp.concatenate([jnp.where(low, qh, zero), jnp.where(low, zero, qh)], axis=0)


def _dot_nt(a, b):
    return lax.dot_general(a, b, (((1,), (1,)), ((), ())), preferred_element_type=F32)


def _update_state(m_scr, l_scr, acc_scr, h, s, pv):
    m = m_scr[h]
    m_new = jnp.maximum(m, jnp.max(s, axis=1, keepdims=True))
    alpha = jnp.exp2(m - m_new)
    p = jnp.exp2(s - m_new)
    l_scr[h] = alpha * l_scr[h] + jnp.sum(p, axis=1, keepdims=True)
    acc_scr[h] = alpha * acc_scr[h] + pv(p.astype(BF16))
    m_scr[h] = m_new


def _state_scratch(n_heads, t):
    return [pltpu.VMEM((n_heads, 2 * t, 1), F32), pltpu.VMEM((n_heads, 2 * t, 1), F32),
            pltpu.VMEM((n_heads, 2 * t, LANES), F32)]


def _diff_combine(on, t, lam, sub_gain, lam_init):
    o = on[:t] - lam * on[t:]
    return _rms(o, sub_gain) * (1.0 - lam_init)


def _select_groups(on, t):
    lane = lax.broadcasted_iota(jnp.int32, (t, LANES), 1)
    return jnp.where(lane < GROUP, on[:t], on[t:])


def _diff_prompt_kernel(qt_ref, k_ref, vt_ref, x_ref, wo_ref, sg_ref, lp_ref, o_ref,
                        qs_scr, m_scr, l_scr, acc_scr, s_scr, mask_scr, obuf, *, lam_init):
    i = pl.program_id(1)
    d, t = qt_ref.shape
    n_heads = d // LANES

    @pl.when((pl.program_id(0) == 0) & (i == 0))
    def _():
        key = lax.broadcasted_iota(jnp.int32, (t, 2 * t), 0)
        qry = lax.broadcasted_iota(jnp.int32, (t, 2 * t), 1) % t
        mask_scr[...] = jnp.where((key // CHUNK) <= (qry // CHUNK), 0.0, NEG)

    for h in range(n_heads):
        qs_scr[h] = _stack_groups_t(qt_ref[_head(h), :])
    _reset_state(m_scr, l_scr, acc_scr)

    def kv_block(j, diagonal):
        rows = pl.ds(pl.multiple_of(j * t, t), t)
        for h in range(n_heads):
            s = jnp.dot(k_ref[rows, _head(h)], qs_scr[h], preferred_element_type=F32)
            s_scr[h] = s + mask_scr[...] if diagonal else s
        for h in range(n_heads):
            _update_state_t(m_scr, l_scr, acc_scr, h, s_scr[h], vt_ref[j, _head(h), :])

    def body(j, c):
        kv_block(j, False)
        return c

    lax.fori_loop(0, i, body, 0)
    kv_block(i, True)
    lam = _lambda_value(lp_ref[...], lam_init)
    for h in range(n_heads):
        on = acc_scr[h] / l_scr[h]
        o = on[:, :t] - lam * on[:, t:]
        o = o * lax.rsqrt(jnp.mean(o * o, axis=0, keepdims=True) + EPS)
        obuf[:, _head(h)] = (o.T * sg_ref[:, _head(h)] * (1.0 - lam_init)).astype(BF16)
    o_ref[...] = x_ref[...] + jnp.dot(obuf[...], wo_ref[...], preferred_element_type=F32)


def _diff_attention_prompt(qt, k, vt, x, w_out, sub_gain, lam_p, *, batch, seq, lam_init):
    rows, d = x.shape
    t = vt.shape[2]
    nq = seq // t
    n_heads = d // LANES
    row_spec = pl.BlockSpec((t, d), lambda b, i: (b * nq + i, 0))
    return pl.pallas_call(
        functools.partial(_diff_prompt_kernel, lam_init=lam_init),
        grid=(batch, nq),
        in_specs=[pl.BlockSpec((None, d, t), lambda b, i: (b, 0, i)),
                  pl.BlockSpec((seq, d), lambda b, i: (b, 0)),
                  pl.BlockSpec((nq, d, t), lambda b, i: (b, 0, 0)),
                  row_spec, _const_spec((d, d)), _const_spec((1, d)), _const_spec(lam_p.shape)],
        out_specs=row_spec,
        out_shape=jax.ShapeDtypeStruct((rows, d), F32),
        scratch_shapes=[pltpu.VMEM((n_heads, LANES, 2 * t), BF16)] + _state_scratch_t(n_heads, t)
        + [pltpu.VMEM((n_heads, t, 2 * t), F32), pltpu.VMEM((t, 2 * t), F32), pltpu.VMEM((t, d), BF16)],
        compiler_params=_params("arbitrary", "arbitrary"),
        name="diff_attention_prompt",
    )(qt, k, vt, x, w_out, sub_gain, lam_p)


def _diff_decode_kernel(q_ref, kc_ref, vc_ref, kn_ref, vn_ref, x_ref, wo_ref, sg_ref, lp_ref, o_ref,
                        m_scr, l_scr, acc_scr, obuf, *, lam_init):
    j = pl.program_id(1)
    t, d = q_ref.shape
    n_heads = d // LANES
    tk = kc_ref.shape[1]

    @pl.when(j == 0)
    def _():
        _reset_state(m_scr, l_scr, acc_scr)

    for h in range(n_heads):
        qs = _stack_groups(q_ref[:, _head(h)])
        s = jnp.dot(qs, kc_ref[_head(h), :].astype(BF16), preferred_element_type=F32)
        vh = vc_ref[pl.ds(h, tk, stride=n_heads), :].astype(BF16)
        _update_state(m_scr, l_scr, acc_scr, h, s, lambda p: jnp.dot(p, vh, preferred_element_type=F32))

    @pl.when(j == pl.num_programs(1) - 1)
    def _():
        lam = _lambda_value(lp_ref[...], lam_init)
        for h in range(n_heads):
            qs = _stack_groups(q_ref[:, _head(h)])
            vh = vn_ref[:, _head(h)]
            _update_state(m_scr, l_scr, acc_scr, h, _dot_nt(qs, kn_ref[:, _head(h)]),
                          lambda p: jnp.dot(p, vh, preferred_element_type=F32))
            on = acc_scr[h] / l_scr[h]
            obuf[:, _head(h)] = _diff_combine(on, t, lam, sg_ref[:, _head(h)], lam_init).astype(BF16)
        o_ref[...] = x_ref[...] + jnp.dot(obuf[...], wo_ref[...], preferred_element_type=F32)


def _diff_attention_decode(q, cache_kt, cache_vh, k_new, v_new, x, w_out, sub_gain, lam_p, *, layer, lam_init):
    _, batch, d, past = cache_kt.shape
    rows = x.shape[0]
    t = rows // batch
    tk = min(DEC_KEY_TILE, past)
    n_heads = d // LANES
    row_spec = pl.BlockSpec((t, d), lambda b, j: (b, 0))
    kt_spec = pl.BlockSpec((None, None, d, tk), lambda b, j: (layer, b, 0, j))
    v_spec = pl.BlockSpec((None, None, tk * n_heads, LANES), lambda b, j: (layer, b, j, 0))
    return pl.pallas_call(
        functools.partial(_diff_decode_kernel, lam_init=lam_init),
        grid=(batch, past // tk),
        in_specs=[row_spec, kt_spec, v_spec, row_spec, row_spec, row_spec, _const_spec((d, d)),
                  _const_spec((1, d)), _const_spec(lam_p.shape)],
        out_specs=row_spec,
        out_shape=jax.ShapeDtypeStruct((rows, d), F32),
        scratch_shapes=_state_scratch(n_heads, t) + [pltpu.VMEM((t, d), BF16)],
        compiler_params=_params("arbitrary", "arbitrary"),
        name="diff_attention_decode",
    )(q, cache_kt, cache_vh, k_new, v_new, x, w_out, sub_gain, lam_p)


def _gather_rows(tbl, idx_of_col, width):
    n_pad = tbl.shape[1]
    t_idx = lax.broadcasted_iota(jnp.int32, (n_pad, width), 0)
    v_idx = lax.broadcasted_iota(jnp.int32, (n_pad, width), 1)
    onehot = jnp.where(t_idx == idx_of_col(v_idx), 1.0, 0.0).astype(BF16)
    hi = tbl.astype(BF16)
    r1 = tbl - hi.astype(F32)
    mid = r1.astype(BF16)
    lo = (r1 - mid.astype(F32)).astype(BF16)
    dot = lambda a: jnp.dot(a, onehot, preferred_element_type=F32)
    return dot(hi) + dot(mid) + dot(lo)


def _rel_index(rel):
    return jnp.clip(rel, -C_REL_CLIP, C_REL_CLIP) + C_REL_CLIP


def _toeplitz(vec, rows, cols, origin):
    width = vec.shape[1]
    xb = jnp.broadcast_to(vec, (rows, width))
    return pltpu.roll(xb, (width - origin) % width, 1, stride=1, stride_axis=0)[:, :cols]


def _bias_kernel(tbl_ref, bp_ref, bs_ref, *, t, n_blocks, ts, win):
    tbl = tbl_ref[...] * LOG2E
    heads = tbl.shape[0]
    key = lax.broadcasted_iota(jnp.int32, (t, t), 0)
    qry = lax.broadcasted_iota(jnp.int32, (t, t), 1)
    for jb in range(n_blocks):
        off = (n_blocks - 1 - jb) * t
        g = _gather_rows(tbl, lambda v: _rel_index(v - (t - 1) + off), 2 * t)
        key_chunk = (jb * t + key) // CHUNK - (n_blocks - 1) * t // CHUNK
        q_chunk = qry // CHUNK
        visible = (key_chunk <= q_chunk) & (key_chunk >= q_chunk - C_LEFT_CHUNKS)
        for h in range(heads):
            tile = _toeplitz(g[h:h + 1, :], t, t, t - 1)
            bp_ref[jb, h // 2, :, (h % 2) * t:(h % 2 + 1) * t] = jnp.where(visible, tile, NEG)
    width = bs_ref.shape[-1]
    wide = 1 << (width + ts).bit_length()
    g = _gather_rows(tbl, lambda v: _rel_index(win + ts - 1 - v), wide)
    for h in range(heads):
        tile = _toeplitz(g[h:h + 1, :], ts, width, ts - 1)
        bs_ref[h // 2, (h % 2) * ts:(h % 2 + 1) * ts, :] = tile


def _band_bias(table_t, *, t, n_blocks, ts, win):
    heads, n = table_t.shape
    n_pad = -(-n // LANES) * LANES
    tbl = jnp.pad(table_t, ((0, 0), (0, n_pad - n)))
    return pl.pallas_call(
        functools.partial(_bias_kernel, t=t, n_blocks=n_blocks, ts=ts, win=win),
        out_shape=[jax.ShapeDtypeStruct((n_blocks, heads // 2, t, 2 * t), F32),
                   jax.ShapeDtypeStruct((heads // 2, 2 * ts, win + ts), F32)],
        compiler_params=pltpu.CompilerParams(vmem_limit_bytes=VMEM_LIMIT_BYTES),
        name="band_bias",
    )(tbl)


def _band_prompt_kernel(qt_ref, k_ref, vt_ref, b_ref, x_ref, wo_ref, o_ref, qs_scr, m_scr, l_scr, acc_scr, s_scr,
                        obuf):
    i = pl.program_id(1)
    d, t = qt_ref.shape
    n_heads = d // LANES
    n_blocks = b_ref.shape[0]
    first = jnp.maximum(n_blocks - 1 - i, 0)
    for h in range(n_heads):
        qs_scr[h] = _stack_groups_t(qt_ref[_head(h), :])
    _reset_state(m_scr, l_scr, acc_scr)

    def body(jb, c):
        j = i - (n_blocks - 1) + jb
        rows = pl.ds(pl.multiple_of(j * t, t), t)
        for h in range(n_heads):
            s_scr[h] = jnp.dot(k_ref[rows, _head(h)], qs_scr[h], preferred_element_type=F32) + b_ref[jb, h]
        for h in range(n_heads):
            _update_state_t(m_scr, l_scr, acc_scr, h, s_scr[h], vt_ref[j, _head(h), :])
        return c

    lax.fori_loop(first, n_blocks, body, 0)
    feat = lax.broadcasted_iota(jnp.int32, (LANES, t), 0)
    for h in range(n_heads):
        on = acc_scr[h] / l_scr[h]
        obuf[:, _head(h)] = jnp.where(feat < GROUP, on[:, :t], on[:, t:]).T.astype(BF16)
    o_ref[...] = x_ref[...] + jnp.dot(obuf[...], wo_ref[...], preferred_element_type=F32)


def _band_attention_prompt(qt, k, vt, bias, x, w_out, *, batch, seq):
    rows, d = x.shape
    t = vt.shape[2]
    nq = seq // t
    n_heads = d // LANES
    row_spec = pl.BlockSpec((t, d), lambda b, i: (b * nq + i, 0))
    return pl.pallas_call(
        _band_prompt_kernel,
        grid=(batch, nq),
        in_specs=[pl.BlockSpec((None, d, t), lambda b, i: (b, 0, i)),
                  pl.BlockSpec((seq, d), lambda b, i: (b, 0)),
                  pl.BlockSpec((nq, d, t), lambda b, i: (b, 0, 0)),
                  _const_spec(bias.shape), row_spec, _const_spec((d, d))],
        out_specs=row_spec,
        out_shape=jax.ShapeDtypeStruct((rows, d), F32),
        scratch_shapes=[pltpu.VMEM((n_heads, LANES, 2 * t), BF16)] + _state_scratch_t(n_heads, t)
        + [pltpu.VMEM((n_heads, t, 2 * t), F32), pltpu.VMEM((t, d), BF16)],
        compiler_params=_params("arbitrary", "arbitrary"),
        name="band_attention_prompt",
    )(qt, k, vt, bias, x, w_out)


def _band_decode_kernel(q_ref, kc_ref, vc_ref, kn_ref, vn_ref, b_ref, x_ref, wo_ref, o_ref, obuf):
    t, d = q_ref.shape
    win = kc_ref.shape[1]
    for h in range(d // LANES):
        qs = _stack_groups(q_ref[:, _head(h)])
        bias = b_ref[h]
        s_c = jnp.dot(qs, kc_ref[_head(h), :].astype(BF16), preferred_element_type=F32) + bias[:, :win]
        s_n = _dot_nt(qs, kn_ref[:, _head(h)]) + bias[:, win:]
        m = jnp.maximum(jnp.max(s_c, axis=1, keepdims=True), jnp.max(s_n, axis=1, keepdims=True))
        p_c = jnp.exp2(s_c - m)
        p_n = jnp.exp2(s_n - m)
        l = jnp.sum(p_c, axis=1, keepdims=True) + jnp.sum(p_n, axis=1, keepdims=True)
        acc = (_dot_nt(p_c.astype(BF16), vc_ref[_head(h), :].astype(BF16))
               + jnp.dot(p_n.astype(BF16), vn_ref[:, _head(h)], preferred_element_type=F32))
        obuf[:, _head(h)] = _select_groups(acc / l, t).astype(BF16)
    o_ref[...] = x_ref[...] + jnp.dot(obuf[...], wo_ref[...], preferred_element_type=F32)


def _band_attention_decode(q, cache_kt, cache_vt, k_new, v_new, bias, x, w_out, *, layer):
    _, batch, d, win = cache_kt.shape
    rows = x.shape[0]
    t = rows // batch
    row_spec = pl.BlockSpec((t, d), lambda b: (b, 0))
    cache_spec = pl.BlockSpec((None, None, d, win), lambda b: (layer, b, 0, 0))
    return pl.pallas_call(
        _band_decode_kernel,
        grid=(batch,),
        in_specs=[row_spec, cache_spec, cache_spec, row_spec, row_spec, _const_spec(bias.shape), row_spec,
                  _const_spec((d, d))],
        out_specs=row_spec,
        out_shape=jax.ShapeDtypeStruct((rows, d), F32),
        scratch_shapes=[pltpu.VMEM((t, d), BF16)],
        compiler_params=_params("arbitrary"),
        name="band_attention_decode",
    )(q, cache_kt, cache_vt, k_new, v_new, bias, x, w_out)


def _gelu_tanh(x):
    y2 = (2.0 * math.sqrt(2.0 / math.pi)) * (x + 0.044715 * (x * x * x))
    return x / (1.0 + jnp.exp(-y2))


def _rglru_kernel(x_ref, g_ref, win_ref, bin_ref, cw_ref, cb_ref, gaw_ref, gab_ref, gxw_ref, gxb_ref,
                  lam_ref, wout_ref, hist_ref, h0_ref, o_ref, cst_ref, hst_ref, ubuf, hcar, a_scr, b_scr):
    tm, w = a_scr.shape[0] - SUBLANES, a_scr.shape[1]
    pad = SUBLANES

    @pl.when(pl.program_id(1) == 0)
    def _():
        ubuf[0:pad, :] = hist_ref[...]
        hcar[...] = h0_ref[...]
        a_scr[0:pad, :] = jnp.zeros((pad, w), F32)
        b_scr[0:pad, :] = jnp.zeros((pad, w), F32)

    x = x_ref[...]
    xn = _rms(x, g_ref[...]).astype(BF16)
    gu = jnp.dot(xn, win_ref[...], preferred_element_type=F32) + bin_ref[...]
    gate = _gelu_tanh(gu[:, :w])
    u = gu[:, w:]
    ubuf[pad:pad + tm, :] = u
    cw = cw_ref[...]
    xc = cb_ref[...] + cw[3:4] * u
    for j in range(B_CONV_W - 1):
        xc = xc + cw[j:j + 1] * ubuf[pad - 3 + j:pad - 3 + j + tm, :]

    xcb = xc.astype(BF16)
    bw = w // B_BLOCKS
    r_parts, i_parts = [], []
    for n in range(B_BLOCKS):
        blk = xcb[:, n * bw:(n + 1) * bw]
        r_parts.append(jnp.dot(blk, gaw_ref[n], preferred_element_type=F32))
        i_parts.append(jnp.dot(blk, gxw_ref[n], preferred_element_type=F32))
    r = jax.nn.sigmoid(jnp.concatenate(r_parts, axis=1) + gab_ref[...])
    ig = jax.nn.sigmoid(jnp.concatenate(i_parts, axis=1) + gxb_ref[...])
    z = -lam_ref[...]
    softplus = jnp.maximum(z, 0.0) + jnp.log1p(jnp.exp(-jnp.abs(z)))
    a = jnp.exp(-B_C * r * softplus)
    b = jnp.sqrt(1.0 - a * a) * (ig * xc)

    sub = lax.broadcasted_iota(jnp.int32, (tm, w), 0) % SUBLANES
    for s in (1, 2, 4):
        a_scr[pad:pad + tm, :] = a
        b_scr[pad:pad + tm, :] = b
        a_prev = a_scr[pad - s:pad - s + tm, :]
        b_prev = b_scr[pad - s:pad - s + tm, :]
        inside = sub >= s
        b = jnp.where(inside, a * b_prev + b, b)
        a = jnp.where(inside, a * a_prev, a)
    h_prev = hcar[...]
    for gidx in range(tm // SUBLANES):
        rows = slice(gidx * SUBLANES, (gidx + 1) * SUBLANES)
        hg = a[rows] * h_prev + b[rows]
        b_scr[pad + gidx * SUBLANES:pad + (gidx + 1) * SUBLANES, :] = hg
        h_prev = jnp.broadcast_to(hg[SUBLANES - 1:SUBLANES, :], (SUBLANES, w))
    hcar[...] = h_prev
    hs = b_scr[pad:pad + tm, :]

    y = jnp.dot((hs * gate).astype(BF16), wout_ref[...], preferred_element_type=F32)
    o_ref[...] = x + y
    ubuf[0:pad, :] = u[tm - pad:tm, :]
    cst_ref[...] = u[tm - pad:tm, :]
    hst_ref[...] = h_prev


def _rglru(x, g, p, hist, h0, *, batch, seq):
    rows, d = x.shape
    w = p["w_out"].shape[0]
    tm = _row_tile(seq)
    nt = seq // tm
    row_spec = pl.BlockSpec((tm, d), lambda b, t: (b * nt + t, 0))
    state_spec = pl.BlockSpec((None, SUBLANES, w), lambda b, t: (b, 0, 0))
    consts = [g, p["w_in"], p["b_in"], p["conv_w"], p["conv_b"], p["ga_w"], p["ga_b"], p["gx_w"], p["gx_b"],
              p["lam"], p["w_out"]]
    return pl.pallas_call(
        _rglru_kernel,
        grid=(batch, nt),
        in_specs=[row_spec] + [_const_spec(c.shape) for c in consts] + [state_spec, state_spec],
        out_specs=[row_spec, state_spec, state_spec],
        out_shape=[jax.ShapeDtypeStruct((rows, d), F32), jax.ShapeDtypeStruct((batch, SUBLANES, w), F32),
                   jax.ShapeDtypeStruct((batch, SUBLANES, w), F32)],
        scratch_shapes=[pltpu.VMEM((tm + SUBLANES, w), F32), pltpu.VMEM((SUBLANES, w), F32),
                        pltpu.VMEM((tm + SUBLANES, w), F32), pltpu.VMEM((tm + SUBLANES, w), F32)],
        compiler_params=_params("arbitrary", "arbitrary"),
        name="rglru_block",
    )(x, *consts, hist, h0)


def _mlp_kernel(x_ref, g_ref, w1_ref, w2_ref, gf_ref, o_ref, *, final_norm):
    x = x_ref[...]
    xn = _rms(x, g_ref[...]).astype(BF16)
    acc = x
    for c in range(w1_ref.shape[1] // FF_CHUNK):
        cols = slice(c * FF_CHUNK, (c + 1) * FF_CHUNK)
        h = jnp.dot(xn, w1_ref[:, cols], preferred_element_type=F32)
        h = jnp.square(jnp.maximum(h, 0.0)).astype(BF16)
        acc = acc + jnp.dot(h, w2_ref[cols, :], preferred_element_type=F32)
    if final_norm:
        acc = _rms(acc, gf_ref[...])
    o_ref[...] = acc


def _mlp(x, g, w1, w2, g_final, *, final_norm):
    rows, d = x.shape
    tm = _row_tile(rows)
    row_spec = pl.BlockSpec((tm, d), lambda i: (i, 0))
    return pl.pallas_call(
        functools.partial(_mlp_kernel, final_norm=final_norm),
        grid=(rows // tm,),
        in_specs=[row_spec, _const_spec((1, d)), _const_spec(w1.shape), _const_spec(w2.shape),
                  _const_spec((1, d))],
        out_specs=row_spec,
        out_shape=jax.ShapeDtypeStruct((rows, d), F32),
        compiler_params=_params("arbitrary"),
        name="mlp",
    )(x, g, w1, w2, g_final)


def _rope_tables(pos):
    half = GROUP // 2
    inv = ROPE_THETA ** (-jnp.arange(half, dtype=F32) / half)
    ang = pos.astype(F32)[:, None] * inv[None, :]
    cos = jnp.tile(jnp.cos(ang), (1, LANES // half))
    sin = jnp.tile(jnp.concatenate([-jnp.sin(ang), jnp.sin(ang)], axis=1), (1, LANES // GROUP))
    return cos, sin


def _feature_major(cache):
    n, batch, pos = cache.shape[:3]
    nd = cache.ndim
    return jnp.transpose(cache, (0, 1) + tuple(range(3, nd)) + (2,)).reshape(n, batch, -1, pos)


def _position_major(x, feature_dims):
    n, batch, _, pos = x.shape
    nf = len(feature_dims)
    return jnp.transpose(x.reshape(n, batch, *feature_dims, pos), (0, 1, 2 + nf) + tuple(range(2, 2 + nf)))


def kernel(x_prompt, x_sample, cache_a_k, cache_a_v, state_b_conv, state_b_h, cache_c_k, cache_c_v, norm_mix_g, norm_mlp_g, norm_final_g, a_w_in, a_q_norm_g, a_k_norm_g, a_lambda, a_subln_g, a_w_out, b_w_in, b_b_in, b_conv_w, b_conv_b, b_gate_a_w, b_gate_a_b, b_gate_x_w, b_gate_x_b, b_lambda, b_w_out, c_w_in, c_q_norm_g, c_k_norm_g, c_rel_bias, c_w_out, mlp_w1, mlp_w2):
    batch, seq, d = x_prompt.shape
    dbatch, dseq, _ = x_sample.shape
    depth = norm_mix_g.shape[0]
    past = cache_a_k.shape[2]
    n_heads = d // LANES
    xp = x_prompt.reshape(batch * seq, d)
    xs = x_sample.reshape(dbatch * dseq, d)
    row = lambda v: v.reshape(1, -1).astype(F32)
    tile_gain = lambda v: jnp.tile(v.astype(F32), d // v.shape[0]).reshape(1, d)
    grp = jnp.arange(LANES) // GROUP
    bd = (grp[:, None] == grp[None, :]).astype(BF16)
    rope_p = _rope_tables(jnp.arange(seq))
    rope_s = tuple(jnp.tile(t, (dbatch, 1)) for t in _rope_tables(past + jnp.arange(dseq)))
    keep = min(C_BAND_PAST, seq)
    att_t = min(ATT_TILE, seq)
    n_band_blocks = C_BAND_PAST // att_t + 1
    a_kt = _feature_major(cache_a_k)
    a_vh = cache_a_v.reshape(cache_a_v.shape[0], dbatch, past * n_heads, LANES)
    c_kt, c_vt = _feature_major(cache_c_k), _feature_major(cache_c_v)
    c_heads = c_rel_bias.shape[2]

    outs = {name: [] for name in ("aks", "avs", "bcp", "bhp", "bcs", "bhs", "cks", "cvs")}
    akp = avp = ckp = cvp = None
    for layer in range(depth):
        kind, idx = layer % 3, layer // 3
        g_mix = row(norm_mix_g[layer])
        if kind == 0:
            lam_init = 0.8 - 0.6 * math.exp(-0.3 * layer)
            w_in = a_w_in[idx].astype(BF16)
            w_out = a_w_out[idx].astype(BF16)
            qg, kg, sg = tile_gain(a_q_norm_g[idx]), tile_gain(a_k_norm_g[idx]), tile_gain(a_subln_g[idx])
            lam_p = a_lambda[idx].astype(F32)
            qt, k, vt, akp, avp = _project(xp, g_mix, w_in, qg, kg, bd, batch=batch, seq=seq, keep=seq,
                                           rope_tables=rope_p, prompt=True, kf_layout="cols",
                                           vf_layout="heads", prev_k=akp, prev_v=avp)
            xp = _diff_attention_prompt(qt, k, vt, xp, w_out, sg, lam_p, batch=batch, seq=seq, lam_init=lam_init)
            q, k, v, kf, vf = _project(xs, g_mix, w_in, qg, kg, bd, batch=dbatch, seq=dseq, keep=dseq,
                                       rope_tables=rope_s)
            xs = _diff_attention_decode(q, a_kt, a_vh, k, v, xs, w_out, sg, lam_p, layer=idx, lam_init=lam_init)
            outs["aks"].append(kf.reshape(dbatch, dseq, n_heads, 2, GROUP))
            outs["avs"].append(vf.reshape(dbatch, dseq, n_heads, LANES))
        elif kind == 1:
            w = b_w_out.shape[1]
            p = dict(w_in=b_w_in[idx].astype(BF16), b_in=row(b_b_in[idx]), conv_w=b_conv_w[idx].astype(F32),
                     conv_b=row(b_conv_b[idx]), ga_w=b_gate_a_w[idx].astype(BF16), ga_b=row(b_gate_a_b[idx]),
                     gx_w=b_gate_x_w[idx].astype(BF16), gx_b=row(b_gate_x_b[idx]), lam=row(b_lambda[idx]),
                     w_out=b_w_out[idx].astype(BF16))
            zeros = jnp.zeros((batch, SUBLANES, w), F32)
            xp, cst, hst = _rglru(xp, g_mix, p, zeros, zeros, batch=batch, seq=seq)
            outs["bcp"].append(cst[:, SUBLANES - (B_CONV_W - 1):])
            outs["bhp"].append(hst[:, 0])
            hist = jnp.pad(state_b_conv[idx].astype(F32), ((0, 0), (SUBLANES - (B_CONV_W - 1), 0), (0, 0)))
            h0 = jnp.broadcast_to(state_b_h[idx].astype(F32)[:, None, :], (dbatch, SUBLANES, w))
            xs, cst, hst = _rglru(xs, g_mix, p, hist, h0, batch=dbatch, seq=dseq)
            outs["bcs"].append(cst[:, SUBLANES - (B_CONV_W - 1):])
            outs["bhs"].append(hst[:, 0])
        else:
            w_in = c_w_in[idx].astype(BF16)
            w_out = c_w_out[idx].astype(BF16)
            qg, kg = tile_gain(c_q_norm_g[idx]), tile_gain(c_k_norm_g[idx])
            win = c_kt.shape[3]
            bias_p, bias_s = _band_bias(c_rel_bias[idx].astype(F32).T, t=att_t, n_blocks=n_band_blocks, ts=dseq,
                                        win=win)
            qt, k, vt, ckp, cvp = _project(xp, g_mix, w_in, qg, kg, bd, batch=batch, seq=seq, keep=keep,
                                           prompt=True, kf_layout="cols", vf_layout="cols", prev_k=ckp,
                                           prev_v=cvp)
            xp = _band_attention_prompt(qt, k, vt, bias_p, xp, w_out, batch=batch, seq=seq)
            q, k, v, kf, vf = _project(xs, g_mix, w_in, qg, kg, bd, batch=dbatch, seq=dseq, keep=dseq)
            xs = _band_attention_decode(q, c_kt, c_vt, k, v, bias_s, xs, w_out, layer=idx)
            outs["cks"].append(kf.reshape(dbatch, dseq, c_heads, GROUP))
            outs["cvs"].append(vf.reshape(dbatch, dseq, c_heads, GROUP))
        g_mlp = row(norm_mlp_g[layer])
        w1 = mlp_w1[layer].astype(BF16)
        w2 = mlp_w2[layer].astype(BF16)
        last = layer == depth - 1
        xp = _mlp(xp, g_mlp, w1, w2, row(norm_final_g), final_norm=last)
        xs = _mlp(xs, g_mlp, w1, w2, row(norm_final_g), final_norm=last)
    st = lambda name: jnp.stack(outs[name])
    return (xp.reshape(batch, seq, d), xs.reshape(dbatch, dseq, d),
            _position_major(akp, (n_heads, 2, GROUP)), avp.reshape(-1, batch, seq, n_heads, LANES),
            st("aks"), st("avs"),
            st("bcp"), st("bhp"), st("bcs"), st("bhs"),
            _position_major(ckp, (c_heads, GROUP)), _position_major(cvp, (c_heads, GROUP)),
            st("cks"), st("cvs"))
```

```python
import functools
import math

import jax
import jax.numpy as jnp
from jax import lax
from jax.experimental import pallas as pl
from jax.experimental.pallas import tpu as pltpu

F32 = jnp.float32
BF16 = jnp.bfloat16

EPS = 1e-6
CHUNK = 64
GROUP = 64
LANES = 128
SUBLANES = 8
ROPE_THETA = 10000.0
LOG2E = 1.0 / math.log(2.0)
Q_SCALE = GROUP ** -0.5 * LOG2E
B_C = 8.0
B_CONV_W = 4
B_BLOCKS = 4
C_LEFT_CHUNKS = 8
C_BAND_PAST = C_LEFT_CHUNKS * CHUNK
C_REL_CLIP = 128
NEG = -1e30
VMEM_LIMIT_BYTES = 56 * 1024 * 1024

ROW_TILE = 512
ATT_TILE = 256
DEC_KEY_TILE = 2048
FF_CHUNK = 512
PROJ_ROWS = 256


def _row_tile(rows, target=ROW_TILE):
    t = min(rows, target)
    assert rows % t == 0, (rows, t)
    return t


def _params(*semantics):
    return pltpu.CompilerParams(dimension_semantics=semantics, vmem_limit_bytes=VMEM_LIMIT_BYTES)


def _const_spec(shape):
    nd = len(shape)
    return pl.BlockSpec(shape, lambda *_: (0,) * nd, pipeline_mode=pl.Buffered(1))


def _rms(x, g):
    ms = jnp.mean(x * x, axis=-1, keepdims=True)
    return x * lax.rsqrt(ms + EPS) * g


def _head(h):
    return slice(h * LANES, (h + 1) * LANES)


def _proj_rows_kernel(*refs, rope):
    it = iter(refs)
    x_ref, g_ref, w_ref, qg_ref, kg_ref, bd_ref = (next(it) for _ in range(6))
    cos_ref, sin_ref = (next(it), next(it)) if rope else (None, None)
    q_ref, k_ref, v_ref, kf_ref, vf_ref = (next(it) for _ in range(5))
    x = x_ref[...]
    tm, d = x.shape
    xn = _rms(x, g_ref[...]).astype(BF16)
    qkv = jnp.dot(xn, w_ref[...], preferred_element_type=F32)
    bd = bd_ref[...]

    def group_norm(t, gain):
        sq = (t * t).astype(BF16)
        ss = jnp.concatenate([jnp.dot(sq[:, _head(j)], bd, preferred_element_type=F32)
                              for j in range(d // LANES)], axis=1)
        return t * lax.rsqrt(ss * (1.0 / GROUP) + EPS) * gain

    def rotary(t):
        cos = cos_ref[...]
        sin = sin_ref[...]
        lane = lax.broadcasted_iota(jnp.int32, (tm, LANES), 1)
        first_half = (lane % GROUP) < (GROUP // 2)
        outs = []
        for j in range(d // LANES):
            tj = t[:, _head(j)]
            partner = jnp.where(first_half, pltpu.roll(tj, LANES - GROUP // 2, 1),
                                pltpu.roll(tj, GROUP // 2, 1))
            outs.append(tj * cos + partner * sin)
        return jnp.concatenate(outs, axis=1)

    q = group_norm(qkv[:, :d], qg_ref[...])
    k = group_norm(qkv[:, d:2 * d], kg_ref[...])
    v = qkv[:, 2 * d:]
    if rope:
        q = rotary(q)
        k = rotary(k)
    q_ref[...] = (q * Q_SCALE).astype(BF16)
    k_ref[...] = k.astype(BF16)
    v_ref[...] = v.astype(BF16)
    kf_ref[...] = k
    vf_ref[...] = v


def _project_rows(x, g, w_in, q_gain, k_gain, bd, rope_tables=None):
    rows, d = x.shape
    tm = _row_tile(rows)
    row_spec = pl.BlockSpec((tm, d), lambda i: (i, 0))
    in_specs = [row_spec, _const_spec((1, d)), _const_spec((d, 3 * d)), _const_spec((1, d)),
                _const_spec((1, d)), _const_spec((LANES, LANES))]
    args = [x, g, w_in, q_gain, k_gain, bd]
    if rope_tables is not None:
        in_specs += [pl.BlockSpec((tm, LANES), lambda i: (i, 0))] * 2
        args += list(rope_tables)
    return pl.pallas_call(
        functools.partial(_proj_rows_kernel, rope=rope_tables is not None),
        grid=(rows // tm,),
        in_specs=in_specs,
        out_specs=[row_spec] * 5,
        out_shape=[jax.ShapeDtypeStruct((rows, d), BF16)] * 3 + [jax.ShapeDtypeStruct((rows, d), F32)] * 2,
        compiler_params=_params("arbitrary"),
        name="qkv_project_rows",
    )(*args)


def _proj_prompt_kernel(*refs, rope, vf_layout, n_prev, n_t, keep_t, att_t):
    it = iter(refs)
    x_ref, g_ref, wt_ref, qg_ref, kg_ref = (next(it) for _ in range(5))
    cos_ref, sin_ref = (next(it), next(it)) if rope else (None, None)
    pk_ref, pv_ref = (next(it), next(it)) if n_prev else (None, None)
    qt_ref, k_ref, vt_ref, kf_ref, vf_ref, kt_scr, vt_scr = (next(it) for _ in range(7))
    x = x_ref[...]
    tm, d = x.shape
    xn = _rms(x, g_ref[...]).astype(BF16)
    half = GROUP // 2

    def project(first_row, n_rows):
        return lax.dot_general(wt_ref[first_row:first_row + n_rows, :], xn, (((1,), (1,)), ((), ())),
                               preferred_element_type=F32)

    def norm_rope(t, gain):
        ss = jnp.sum(t * t, axis=0, keepdims=True)
        tn = t * lax.rsqrt(ss * (1.0 / GROUP) + EPS) * gain
        if rope:
            cos, sin = cos_ref[...], sin_ref[...]
            x1, x2 = tn[:half], tn[half:]
            tn = jnp.concatenate([x1 * cos - x2 * sin, x2 * cos + x1 * sin], axis=0)
        return tn

    for c0 in range(0, d, PROJ_ROWS):
        q_c, k_c = project(c0, PROJ_ROWS), project(d + c0, PROJ_ROWS)
        vt_scr[c0:c0 + PROJ_ROWS, :] = project(2 * d + c0, PROJ_ROWS)
        for gi in range(PROJ_ROWS // GROUP):
            loc = slice(gi * GROUP, (gi + 1) * GROUP)
            rows = slice(c0 + gi * GROUP, c0 + (gi + 1) * GROUP)
            qt_ref[rows, :] = (norm_rope(q_c[loc], qg_ref[rows, :]) * Q_SCALE).astype(BF16)
            kt_scr[rows, :] = norm_rope(k_c[loc], kg_ref[rows, :])
    k_ref[...] = kt_scr[...].T.astype(BF16)
    vt = vt_scr[...]
    for c in range(tm // att_t):
        vt_ref[c] = vt[:, c * att_t:(c + 1) * att_t].astype(BF16)

    def write_f32():
        if n_prev:
            kf_ref[0:n_prev] = pk_ref[...]
            vf_ref[0:n_prev] = pv_ref[...]
        kf_ref[n_prev] = kt_scr[...]
        if vf_layout == "heads":
            v = vt.T
            vf_new = vf_ref.at[n_prev]
            for h in range(d // LANES):
                vf_new[pl.ds(h, tm, stride=d // LANES), :] = v[:, _head(h)]
        else:
            vf_ref[n_prev] = vt

    if keep_t < n_t:
        pl.when(pl.program_id(0) % n_t >= n_t - keep_t)(write_f32)
    else:
        write_f32()


def _project_prompt(x, g, w_in_t, q_gain, k_gain, *, batch, seq, keep, rope_tables=None, vf_layout="cols",
                    prev_k=None, prev_v=None):
    rows, d = x.shape
    n_heads = d // LANES
    tm = _row_tile(math.gcd(seq, keep))
    n_t, keep_t = seq // tm, keep // tm
    att_t = min(ATT_TILE, tm)
    n_prev = 0 if prev_k is None else prev_k.shape[0]
    n = n_prev + 1

    kept_tile = lambda i: jnp.maximum(i % n_t - (n_t - keep_t), 0)
    cols_spec = lambda m: pl.BlockSpec((m, None, d, tm), lambda i: (0, i // n_t, 0, kept_tile(i)))
    heads_spec = lambda m: pl.BlockSpec((m, tm * n_heads, LANES),
                                        lambda i: (0, (i // n_t) * keep_t + kept_tile(i), 0))
    v_f32_spec = heads_spec if vf_layout == "heads" else cols_spec
    v_f32_shape = (n, batch * keep * n_heads, LANES) if vf_layout == "heads" else (n, batch, d, keep)

    row_spec = pl.BlockSpec((tm, d), lambda i: (i, 0))
    in_specs = [row_spec, _const_spec((1, d)), _const_spec((3 * d, d)), _const_spec((d, tm)), _const_spec((d, tm))]
    args = [x, g, w_in_t, q_gain, k_gain]
    if rope_tables is not None:
        cos, sin = rope_tables
        n_rt = cos.shape[1] // tm
        in_specs += [pl.BlockSpec((GROUP // 2, tm), lambda i: (0, i % n_rt))] * 2
        args += [cos, sin]
    if n_prev:
        in_specs += [cols_spec(n_prev), v_f32_spec(n_prev)]
        args += [prev_k, prev_v]
    return pl.pallas_call(
        functools.partial(_proj_prompt_kernel, rope=rope_tables is not None, vf_layout=vf_layout, n_prev=n_prev,
                          n_t=n_t, keep_t=keep_t, att_t=att_t),
        grid=(rows // tm,),
        in_specs=in_specs,
        out_specs=[pl.BlockSpec((None, d, tm), lambda i: (i // n_t, 0, i % n_t)), row_spec,
                   pl.BlockSpec((tm // att_t, d, att_t), lambda i: (i, 0, 0)), cols_spec(n), v_f32_spec(n)],
        out_shape=[jax.ShapeDtypeStruct((batch, d, seq), BF16), jax.ShapeDtypeStruct((rows, d), BF16),
                   jax.ShapeDtypeStruct((rows // att_t, d, att_t), BF16),
                   jax.ShapeDtypeStruct((n, batch, d, keep), F32), jax.ShapeDtypeStruct(v_f32_shape, F32)],
        scratch_shapes=[pltpu.VMEM((d, tm), F32), pltpu.VMEM((d, tm), F32)],
        compiler_params=_params("arbitrary"),
        name="qkv_project_prompt",
    )(*args)


def _lambda_value(lp, lam_init):
    return (jnp.exp(jnp.sum(lp[0:1] * lp[1:2], keepdims=True))
            - jnp.exp(jnp.sum(lp[2:3] * lp[3:4], keepdims=True)) + lam_init)


def _reset_state(m_scr, l_scr, acc_scr):
    m_scr[...] = jnp.full(m_scr.shape, NEG, F32)
    l_scr[...] = jnp.zeros(l_scr.shape, F32)
    acc_scr[...] = jnp.zeros(acc_scr.shape, F32)


def _stack_groups_t(qt):
    feat = lax.broadcasted_iota(jnp.int32, qt.shape, 0)
    zero = jnp.zeros_like(qt)
    low = feat < GROUP
    return jnp.concatenate([jnp.where(low, qt, zero), jnp.where(low, zero, qt)], axis=1)


def _update_state_t(m_scr, l_scr, acc_scr, h, s, vt):
    m = m_scr[h]
    m_new = jnp.maximum(m, jnp.max(s, axis=0, keepdims=True))
    alpha = jnp.exp2(m - m_new)
    p = jnp.exp2(s - m_new)
    l_scr[h] = alpha * l_scr[h] + jnp.sum(p, axis=0, keepdims=True)
    acc_scr[h] = alpha * acc_scr[h] + jnp.dot(vt, p.astype(BF16), preferred_element_type=F32)
    m_scr[h] = m_new


def _state_scratch_t(n_heads, t):
    return [pltpu.VMEM((n_heads, 1, 2 * t), F32), pltpu.VMEM((n_heads, 1, 2 * t), F32),
            pltpu.VMEM((n_heads, LANES, 2 * t), F32)]


def _stack_groups(qh):
    lane = lax.broadcasted_iota(jnp.int32, qh.shape, 1)
    zero = jnp.zeros_like(qh)
    low = lane < GROUP
    return jnp.concatenate([jnp.where(low, qh, zero), jnp.where(low, zero, qh)], axis=0)


def _dot_nt(a, b):
    return lax.dot_general(a, b, (((1,), (1,)), ((), ())), preferred_element_type=F32)


def _update_state(m_scr, l_scr, acc_scr, h, s, pv):
    m = m_scr[h]
    m_new = jnp.maximum(m, jnp.max(s, axis=1, keepdims=True))
    alpha = jnp.exp2(m - m_new)
    p = jnp.exp2(s - m_new)
    l_scr[h] = alpha * l_scr[h] + jnp.sum(p, axis=1, keepdims=True)
    acc_scr[h] = alpha * acc_scr[h] + pv(p.astype(BF16))
    m_scr[h] = m_new


def _state_scratch(n_heads, t):
    return [pltpu.VMEM((n_heads, 2 * t, 1), F32), pltpu.VMEM((n_heads, 2 * t, 1), F32),
            pltpu.VMEM((n_heads, 2 * t, LANES), F32)]


def _diff_combine(on, t, lam, sub_gain, lam_init):
    o = on[:t] - lam * on[t:]
    return _rms(o, sub_gain) * (1.0 - lam_init)


def _select_groups(on, t):
    lane = lax.broadcasted_iota(jnp.int32, (t, LANES), 1)
    return jnp.where(lane < GROUP, on[:t], on[t:])


def _diff_prompt_kernel(qt_ref, k_ref, vt_ref, x_ref, wo_ref, sg_ref, lp_ref, o_ref,
                        qs_scr, m_scr, l_scr, acc_scr, s_scr, mask_scr, obuf, *, lam_init):
    i = pl.program_id(1)
    d, t = qt_ref.shape
    n_heads = d // LANES

    @pl.when((pl.program_id(0) == 0) & (i == 0))
    def _():
        key = lax.broadcasted_iota(jnp.int32, (t, 2 * t), 0)
        qry = lax.broadcasted_iota(jnp.int32, (t, 2 * t), 1) % t
        mask_scr[...] = jnp.where((key // CHUNK) <= (qry // CHUNK), 0.0, NEG)

    for h in range(n_heads):
        qs_scr[h] = _stack_groups_t(qt_ref[_head(h), :])
    _reset_state(m_scr, l_scr, acc_scr)

    def kv_block(j, diagonal):
        rows = pl.ds(pl.multiple_of(j * t, t), t)
        for h in range(n_heads):
            s = jnp.dot(k_ref[rows, _head(h)], qs_scr[h], preferred_element_type=F32)
            s_scr[h] = s + mask_scr[...] if diagonal else s
        for h in range(n_heads):
            _update_state_t(m_scr, l_scr, acc_scr, h, s_scr[h], vt_ref[j, _head(h), :])

    def body(j, c):
        kv_block(j, False)
        return c

    lax.fori_loop(0, i, body, 0)
    kv_block(i, True)
    lam = _lambda_value(lp_ref[...], lam_init)
    for h in range(n_heads):
        on = acc_scr[h] * (1.0 / l_scr[h])
        o = on[:, :t] - lam * on[:, t:]
        o = o * lax.rsqrt(jnp.mean(o * o, axis=0, keepdims=True) + EPS)
        obuf[:, _head(h)] = (o.T * sg_ref[:, _head(h)] * (1.0 - lam_init)).astype(BF16)
    o_ref[...] = x_ref[...] + jnp.dot(obuf[...], wo_ref[...], preferred_element_type=F32)


def _diff_attention_prompt(qt, k, vt, x, w_out, sub_gain, lam_p, *, batch, seq, lam_init):
    rows, d = x.shape
    t = vt.shape[2]
    nq = seq // t
    n_heads = d // LANES
    row_spec = pl.BlockSpec((t, d), lambda b, i: (b * nq + i, 0))
    return pl.pallas_call(
        functools.partial(_diff_prompt_kernel, lam_init=lam_init),
        grid=(batch, nq),
        in_specs=[pl.BlockSpec((None, d, t), lambda b, i: (b, 0, i)),
                  pl.BlockSpec((seq, d), lambda b, i: (b, 0)),
                  pl.BlockSpec((nq, d, t), lambda b, i: (b, 0, 0)),
                  row_spec, _const_spec((d, d)), _const_spec((1, d)), _const_spec(lam_p.shape)],
        out_specs=row_spec,
        out_shape=jax.ShapeDtypeStruct((rows, d), F32),
        scratch_shapes=[pltpu.VMEM((n_heads, LANES, 2 * t), BF16)] + _state_scratch_t(n_heads, t)
        + [pltpu.VMEM((n_heads, t, 2 * t), F32), pltpu.VMEM((t, 2 * t), F32), pltpu.VMEM((t, d), BF16)],
        compiler_params=_params("arbitrary", "arbitrary"),
        name="diff_attention_prompt",
    )(qt, k, vt, x, w_out, sub_gain, lam_p)


def _diff_decode_kernel(q_ref, kc_ref, vc_ref, kn_ref, vn_ref, x_ref, wo_ref, sg_ref, lp_ref, o_ref,
                        m_scr, l_scr, acc_scr, obuf, *, lam_init):
    j = pl.program_id(1)
    t, d = q_ref.shape
    n_heads = d // LANES
    tk = kc_ref.shape[1]

    @pl.when(j == 0)
    def _():
        _reset_state(m_scr, l_scr, acc_scr)

    for h in range(n_heads):
        qs = _stack_groups(q_ref[:, _head(h)])
        s = jnp.dot(qs, kc_ref[_head(h), :].astype(BF16), preferred_element_type=F32)
        vh = vc_ref[pl.ds(h, tk, stride=n_heads), :].astype(BF16)
        _update_state(m_scr, l_scr, acc_scr, h, s, lambda p: jnp.dot(p, vh, preferred_element_type=F32))

    @pl.when(j == pl.num_programs(1) - 1)
    def _():
        lam = _lambda_value(lp_ref[...], lam_init)
        for h in range(n_heads):
            qs = _stack_groups(q_ref[:, _head(h)])
            vh = vn_ref[:, _head(h)]
            _update_state(m_scr, l_scr, acc_scr, h, _dot_nt(qs, kn_ref[:, _head(h)]),
                          lambda p: jnp.dot(p, vh, preferred_element_type=F32))
            on = acc_scr[h] / l_scr[h]
            obuf[:, _head(h)] = _diff_combine(on, t, lam, sg_ref[:, _head(h)], lam_init).astype(BF16)
        o_ref[...] = x_ref[...] + jnp.dot(obuf[...], wo_ref[...], preferred_element_type=F32)


def _diff_attention_decode(q, cache_kt, cache_vh, k_new, v_new, x, w_out, sub_gain, lam_p, *, layer, lam_init):
    _, batch, d, past = cache_kt.shape
    rows = x.shape[0]
    t = rows // batch
    tk = min(DEC_KEY_TILE, past)
    n_heads = d // LANES
    row_spec = pl.BlockSpec((t, d), lambda b, j: (b, 0))
    kt_spec = pl.BlockSpec((None, None, d, tk), lambda b, j: (layer, b, 0, j))
    v_spec = pl.BlockSpec((None, None, tk * n_heads, LANES), lambda b, j: (layer, b, j, 0))
    return pl.pallas_call(
        functools.partial(_diff_decode_kernel, lam_init=lam_init),
        grid=(batch, past // tk),
        in_specs=[row_spec, kt_spec, v_spec, row_spec, row_spec, row_spec, _const_spec((d, d)),
                  _const_spec((1, d)), _const_spec(lam_p.shape)],
        out_specs=row_spec,
        out_shape=jax.ShapeDtypeStruct((rows, d), F32),
        scratch_shapes=_state_scratch(n_heads, t) + [pltpu.VMEM((t, d), BF16)],
        compiler_params=_params("arbitrary", "arbitrary"),
        name="diff_attention_decode",
    )(q, cache_kt, cache_vh, k_new, v_new, x, w_out, sub_gain, lam_p)


def _gather_rows(tbl, idx_of_col, width):
    n_pad = tbl.shape[1]
    t_idx = lax.broadcasted_iota(jnp.int32, (n_pad, width), 0)
    v_idx = lax.broadcasted_iota(jnp.int32, (n_pad, width), 1)
    onehot = jnp.where(t_idx == idx_of_col(v_idx), 1.0, 0.0).astype(BF16)
    hi = tbl.astype(BF16)
    r1 = tbl - hi.astype(F32)
    mid = r1.astype(BF16)
    lo = (r1 - mid.astype(F32)).astype(BF16)
    dot = lambda a: jnp.dot(a, onehot, preferred_element_type=F32)
    return dot(hi) + dot(mid) + dot(lo)


def _rel_index(rel):
    return jnp.clip(rel, -C_REL_CLIP, C_REL_CLIP) + C_REL_CLIP


def _toeplitz(vec, rows, cols, origin):
    width = vec.shape[1]
    xb = jnp.broadcast_to(vec, (rows, width))
    return pltpu.roll(xb, (width - origin) % width, 1, stride=1, stride_axis=0)[:, :cols]


def _bias_kernel(tbl_ref, bp_ref, bs_ref, *, t, n_blocks, ts, win):
    tbl = tbl_ref[...] * LOG2E
    heads = tbl.shape[0]
    key = lax.broadcasted_iota(jnp.int32, (t, t), 0)
    qry = lax.broadcasted_iota(jnp.int32, (t, t), 1)
    for jb in range(n_blocks):
        off = (n_blocks - 1 - jb) * t
        g = _gather_rows(tbl, lambda v: _rel_index(v - (t - 1) + off), 2 * t)
        key_chunk = (jb * t + key) // CHUNK - (n_blocks - 1) * t // CHUNK
        q_chunk = qry // CHUNK
        visible = (key_chunk <= q_chunk) & (key_chunk >= q_chunk - C_LEFT_CHUNKS)
        for h in range(heads):
            tile = _toeplitz(g[h:h + 1, :], t, t, t - 1)
            bp_ref[jb, h // 2, :, (h % 2) * t:(h % 2 + 1) * t] = jnp.where(visible, tile, NEG)
    width = bs_ref.shape[-1]
    wide = 1 << (width + ts).bit_length()
    g = _gather_rows(tbl, lambda v: _rel_index(win + ts - 1 - v), wide)
    for h in range(heads):
        tile = _toeplitz(g[h:h + 1, :], ts, width, ts - 1)
        bs_ref[h // 2, (h % 2) * ts:(h % 2 + 1) * ts, :] = tile


def _band_bias(table_t, *, t, n_blocks, ts, win):
    heads, n = table_t.shape
    n_pad = -(-n // LANES) * LANES
    tbl = jnp.pad(table_t, ((0, 0), (0, n_pad - n)))
    return pl.pallas_call(
        functools.partial(_bias_kernel, t=t, n_blocks=n_blocks, ts=ts, win=win),
        out_shape=[jax.ShapeDtypeStruct((n_blocks, heads // 2, t, 2 * t), F32),
                   jax.ShapeDtypeStruct((heads // 2, 2 * ts, win + ts), F32)],
        compiler_params=pltpu.CompilerParams(vmem_limit_bytes=VMEM_LIMIT_BYTES),
        name="band_bias",
    )(tbl)


def _band_prompt_kernel(qt_ref, k_ref, vt_ref, b_ref, x_ref, wo_ref, o_ref, qs_scr, m_scr, l_scr, acc_scr, s_scr,
                        obuf):
    i = pl.program_id(1)
    d, t = qt_ref.shape
    n_heads = d // LANES
    n_blocks = b_ref.shape[0]
    first = jnp.maximum(n_blocks - 1 - i, 0)
    for h in range(n_heads):
        qs_scr[h] = _stack_groups_t(qt_ref[_head(h), :])
    _reset_state(m_scr, l_scr, acc_scr)

    def body(jb, c):
        j = i - (n_blocks - 1) + jb
        rows = pl.ds(pl.multiple_of(j * t, t), t)
        for h in range(n_heads):
            s_scr[h] = jnp.dot(k_ref[rows, _head(h)], qs_scr[h], preferred_element_type=F32) + b_ref[jb, h]
        for h in range(n_heads):
            _update_state_t(m_scr, l_scr, acc_scr, h, s_scr[h], vt_ref[j, _head(h), :])
        return c

    lax.fori_loop(first, n_blocks, body, 0)
    feat = lax.broadcasted_iota(jnp.int32, (LANES, t), 0)
    for h in range(n_heads):
        on = acc_scr[h] * (1.0 / l_scr[h])
        obuf[:, _head(h)] = jnp.where(feat < GROUP, on[:, :t], on[:, t:]).T.astype(BF16)
    o_ref[...] = x_ref[...] + jnp.dot(obuf[...], wo_ref[...], preferred_element_type=F32)


def _band_attention_prompt(qt, k, vt, bias, x, w_out, *, batch, seq):
    rows, d = x.shape
    t = vt.shape[2]
    nq = seq // t
    n_heads = d // LANES
    row_spec = pl.BlockSpec((t, d), lambda b, i: (b * nq + i, 0))
    return pl.pallas_call(
        _band_prompt_kernel,
        grid=(batch, nq),
        in_specs=[pl.BlockSpec((None, d, t), lambda b, i: (b, 0, i)),
                  pl.BlockSpec((seq, d), lambda b, i: (b, 0)),
                  pl.BlockSpec((nq, d, t), lambda b, i: (b, 0, 0)),
                  _const_spec(bias.shape), row_spec, _const_spec((d, d))],
        out_specs=row_spec,
        out_shape=jax.ShapeDtypeStruct((rows, d), F32),
        scratch_shapes=[pltpu.VMEM((n_heads, LANES, 2 * t), BF16)] + _state_scratch_t(n_heads, t)
        + [pltpu.VMEM((n_heads, t, 2 * t), F32), pltpu.VMEM((t, d), BF16)],
        compiler_params=_params("arbitrary", "arbitrary"),
        name="band_attention_prompt",
    )(qt, k, vt, bias, x, w_out)


def _band_decode_kernel(q_ref, kc_ref, vc_ref, kn_ref, vn_ref, b_ref, x_ref, wo_ref, o_ref, obuf):
    t, d = q_ref.shape
    win = kc_ref.shape[1]
    for h in range(d // LANES):
        qs = _stack_groups(q_ref[:, _head(h)])
        bias = b_ref[h]
        s_c = jnp.dot(qs, kc_ref[_head(h), :].astype(BF16), preferred_element_type=F32) + bias[:, :win]
        s_n = _dot_nt(qs, kn_ref[:, _head(h)]) + bias[:, win:]
        m = jnp.maximum(jnp.max(s_c, axis=1, keepdims=True), jnp.max(s_n, axis=1, keepdims=True))
        p_c = jnp.exp2(s_c - m)
        p_n = jnp.exp2(s_n - m)
        l = jnp.sum(p_c, axis=1, keepdims=True) + jnp.sum(p_n, axis=1, keepdims=True)
        acc = (_dot_nt(p_c.astype(BF16), vc_ref[_head(h), :].astype(BF16))
               + jnp.dot(p_n.astype(BF16), vn_ref[:, _head(h)], preferred_element_type=F32))
        obuf[:, _head(h)] = _select_groups(acc / l, t).astype(BF16)
    o_ref[...] = x_ref[...] + jnp.dot(obuf[...], wo_ref[...], preferred_element_type=F32)


def _band_attention_decode(q, cache_kt, cache_vt, k_new, v_new, bias, x, w_out, *, layer):
    _, batch, d, win = cache_kt.shape
    rows = x.shape[0]
    t = rows // batch
    row_spec = pl.BlockSpec((t, d), lambda b: (b, 0))
    cache_spec = pl.BlockSpec((None, None, d, win), lambda b: (layer, b, 0, 0))
    return pl.pallas_call(
        _band_decode_kernel,
        grid=(batch,),
        in_specs=[row_spec, cache_spec, cache_spec, row_spec, row_spec, _const_spec(bias.shape), row_spec,
                  _const_spec((d, d))],
        out_specs=row_spec,
        out_shape=jax.ShapeDtypeStruct((rows, d), F32),
        scratch_shapes=[pltpu.VMEM((t, d), BF16)],
        compiler_params=_params("arbitrary"),
        name="band_attention_decode",
    )(q, cache_kt, cache_vt, k_new, v_new, bias, x, w_out)


def _gelu_tanh(x):
    c = -2.0 * math.sqrt(2.0 / math.pi) * LOG2E
    return x / (1.0 + jnp.exp2(x * (c + (0.044715 * c) * (x * x))))


def _rglru_kernel(x_ref, g_ref, win_ref, bin_ref, cw_ref, cb_ref, gaw_ref, gab_ref, gxw_ref, gxb_ref,
                  lam_ref, wout_ref, hist_ref, h0_ref, o_ref, cst_ref, hst_ref, ubuf, hcar, a_scr, b_scr):
    tm, w = a_scr.shape[0] - SUBLANES, a_scr.shape[1]
    pad = SUBLANES

    @pl.when(pl.program_id(1) == 0)
    def _():
        ubuf[0:pad, :] = hist_ref[...]
        hcar[...] = h0_ref[...]
        a_scr[0:pad, :] = jnp.zeros((pad, w), F32)
        b_scr[0:pad, :] = jnp.zeros((pad, w), F32)

    x = x_ref[...]
    xn = _rms(x, g_ref[...]).astype(BF16)
    gu = jnp.dot(xn, win_ref[...], preferred_element_type=F32) + bin_ref[...]
    gate = _gelu_tanh(gu[:, :w])
    u = gu[:, w:]
    ubuf[pad:pad + tm, :] = u
    cw = cw_ref[...]
    xc = cb_ref[...] + cw[3:4] * u
    for j in range(B_CONV_W - 1):
        xc = xc + cw[j:j + 1] * ubuf[pad - 3 + j:pad - 3 + j + tm, :]

    xcb = xc.astype(BF16)
    bw = w // B_BLOCKS
    r_parts, i_parts = [], []
    for n in range(B_BLOCKS):
        blk = xcb[:, n * bw:(n + 1) * bw]
        r_parts.append(jnp.dot(blk, gaw_ref[n], preferred_element_type=F32))
        i_parts.append(jnp.dot(blk, gxw_ref[n], preferred_element_type=F32))
    r = jax.nn.sigmoid(jnp.concatenate(r_parts, axis=1) + gab_ref[...])
    ig = jax.nn.sigmoid(jnp.concatenate(i_parts, axis=1) + gxb_ref[...])
    z = -lam_ref[...]
    softplus = jnp.maximum(z, 0.0) + jnp.log1p(jnp.exp(-jnp.abs(z)))
    a = jnp.exp(-B_C * r * softplus)
    b = jnp.sqrt(1.0 - a * a) * (ig * xc)

    sub = lax.broadcasted_iota(jnp.int32, (tm, w), 0) % SUBLANES
    for s in (1, 2, 4):
        a_scr[pad:pad + tm, :] = a
        b_scr[pad:pad + tm, :] = b
        a_prev = a_scr[pad - s:pad - s + tm, :]
        b_prev = b_scr[pad - s:pad - s + tm, :]
        inside = sub >= s
        b = jnp.where(inside, a * b_prev + b, b)
        a = jnp.where(inside, a * a_prev, a)
    h_prev = hcar[...]
    for gidx in range(tm // SUBLANES):
        rows = slice(gidx * SUBLANES, (gidx + 1) * SUBLANES)
        hg = a[rows] * h_prev + b[rows]
        b_scr[pad + gidx * SUBLANES:pad + (gidx + 1) * SUBLANES, :] = hg
        h_prev = jnp.broadcast_to(hg[SUBLANES - 1:SUBLANES, :], (SUBLANES, w))
    hcar[...] = h_prev
    hs = b_scr[pad:pad + tm, :]

    y = jnp.dot((hs * gate).astype(BF16), wout_ref[...], preferred_element_type=F32)
    o_ref[...] = x + y
    ubuf[0:pad, :] = u[tm - pad:tm, :]
    cst_ref[...] = u[tm - pad:tm, :]
    hst_ref[...] = h_prev


def _rglru(x, g, p, hist, h0, *, batch, seq):
    rows, d = x.shape
    w = p["w_out"].shape[0]
    tm = _row_tile(seq)
    nt = seq // tm
    row_spec = pl.BlockSpec((tm, d), lambda b, t: (b * nt + t, 0))
    state_spec = pl.BlockSpec((None, SUBLANES, w), lambda b, t: (b, 0, 0))
    consts = [g, p["w_in"], p["b_in"], p["conv_w"], p["conv_b"], p["ga_w"], p["ga_b"], p["gx_w"], p["gx_b"],
              p["lam"], p["w_out"]]
    return pl.pallas_call(
        _rglru_kernel,
        grid=(batch, nt),
        in_specs=[row_spec] + [_const_spec(c.shape) for c in consts] + [state_spec, state_spec],
        out_specs=[row_spec, state_spec, state_spec],
        out_shape=[jax.ShapeDtypeStruct((rows, d), F32), jax.ShapeDtypeStruct((batch, SUBLANES, w), F32),
                   jax.ShapeDtypeStruct((batch, SUBLANES, w), F32)],
        scratch_shapes=[pltpu.VMEM((tm + SUBLANES, w), F32), pltpu.VMEM((SUBLANES, w), F32),
                        pltpu.VMEM((tm + SUBLANES, w), F32), pltpu.VMEM((tm + SUBLANES, w), F32)],
        compiler_params=_params("arbitrary", "arbitrary"),
        name="rglru_block",
    )(x, *consts, hist, h0)


def _mlp_kernel(x_ref, g_ref, w1_ref, w2_ref, gf_ref, o_ref, *, final_norm):
    x = x_ref[...]
    xn = _rms(x, g_ref[...]).astype(BF16)
    acc = x
    for c in range(w1_ref.shape[1] // FF_CHUNK):
        cols = slice(c * FF_CHUNK, (c + 1) * FF_CHUNK)
        h = jnp.dot(xn, w1_ref[:, cols], preferred_element_type=F32)
        h = jnp.square(jnp.maximum(h, 0.0)).astype(BF16)
        acc = acc + jnp.dot(h, w2_ref[cols, :], preferred_element_type=F32)
    if final_norm:
        acc = _rms(acc, gf_ref[...])
    o_ref[...] = acc


def _mlp(x, g, w1, w2, g_final, *, final_norm):
    rows, d = x.shape
    tm = _row_tile(rows)
    row_spec = pl.BlockSpec((tm, d), lambda i: (i, 0))
    return pl.pallas_call(
        functools.partial(_mlp_kernel, final_norm=final_norm),
        grid=(rows // tm,),
        in_specs=[row_spec, _const_spec((1, d)), _const_spec(w1.shape), _const_spec(w2.shape),
                  _const_spec((1, d))],
        out_specs=row_spec,
        out_shape=jax.ShapeDtypeStruct((rows, d), F32),
        compiler_params=_params("arbitrary"),
        name="mlp",
    )(x, g, w1, w2, g_final)


def _rope_angles(pos):
    half = GROUP // 2
    inv = ROPE_THETA ** (-jnp.arange(half, dtype=F32) / half)
    return pos.astype(F32)[:, None] * inv[None, :]


def _rope_tables_rows(pos):
    ang = _rope_angles(pos)
    cos = jnp.tile(jnp.cos(ang), (1, LANES // ang.shape[1]))
    sin = jnp.tile(jnp.concatenate([-jnp.sin(ang), jnp.sin(ang)], axis=1), (1, LANES // GROUP))
    return cos, sin


def _rope_tables_cols(pos):
    ang = _rope_angles(pos).T
    return jnp.cos(ang), jnp.sin(ang)


def _feature_major(cache):
    n, batch, pos = cache.shape[:3]
    nd = cache.ndim
    return jnp.transpose(cache, (0, 1) + tuple(range(3, nd)) + (2,)).reshape(n, batch, -1, pos)


def _position_major(x, feature_dims):
    n, batch, _, pos = x.shape
    nf = len(feature_dims)
    return jnp.transpose(x.reshape(n, batch, *feature_dims, pos), (0, 1, 2 + nf) + tuple(range(2, 2 + nf)))


def kernel(x_prompt, x_sample, cache_a_k, cache_a_v, state_b_conv, state_b_h, cache_c_k, cache_c_v, norm_mix_g, norm_mlp_g, norm_final_g, a_w_in, a_q_norm_g, a_k_norm_g, a_lambda, a_subln_g, a_w_out, b_w_in, b_b_in, b_conv_w, b_conv_b, b_gate_a_w, b_gate_a_b, b_gate_x_w, b_gate_x_b, b_lambda, b_w_out, c_w_in, c_q_norm_g, c_k_norm_g, c_rel_bias, c_w_out, mlp_w1, mlp_w2):
    batch, seq, d = x_prompt.shape
    dbatch, dseq, _ = x_sample.shape
    depth = norm_mix_g.shape[0]
    past = cache_a_k.shape[2]
    n_heads = d // LANES
    xp = x_prompt.reshape(batch * seq, d)
    xs = x_sample.reshape(dbatch * dseq, d)
    row = lambda v: v.reshape(1, -1).astype(F32)
    tile_gain = lambda v: jnp.tile(v.astype(F32), d // v.shape[0]).reshape(1, d)
    grp = jnp.arange(LANES) // GROUP
    bd = (grp[:, None] == grp[None, :]).astype(BF16)
    col_gain = lambda v: jnp.broadcast_to(tile_gain(v).reshape(d, 1), (d, min(ROW_TILE, seq)))
    rope_p = _rope_tables_cols(jnp.arange(seq))
    rope_s = tuple(jnp.tile(t, (dbatch, 1)) for t in _rope_tables_rows(past + jnp.arange(dseq)))
    keep = min(C_BAND_PAST, seq)
    att_t = min(ATT_TILE, seq)
    n_band_blocks = C_BAND_PAST // att_t + 1
    a_kt = _feature_major(cache_a_k)
    a_vh = cache_a_v.reshape(cache_a_v.shape[0], dbatch, past * n_heads, LANES)
    c_kt, c_vt = _feature_major(cache_c_k), _feature_major(cache_c_v)
    c_heads = c_rel_bias.shape[2]

    outs = {name: [] for name in ("aks", "avs", "bcp", "bhp", "bcs", "bhs", "cks", "cvs")}
    akp = avp = ckp = cvp = None
    for layer in range(depth):
        kind, idx = layer % 3, layer // 3
        g_mix = row(norm_mix_g[layer])
        if kind == 0:
            lam_init = 0.8 - 0.6 * math.exp(-0.3 * layer)
            w_in = a_w_in[idx].astype(BF16)
            w_out = a_w_out[idx].astype(BF16)
            qg, kg, sg = tile_gain(a_q_norm_g[idx]), tile_gain(a_k_norm_g[idx]), tile_gain(a_subln_g[idx])
            lam_p = a_lambda[idx].astype(F32)
            qt, k, vt, akp, avp = _project_prompt(xp, g_mix, w_in.T, col_gain(a_q_norm_g[idx]),
                                                  col_gain(a_k_norm_g[idx]), batch=batch, seq=seq, keep=seq,
                                                  rope_tables=rope_p, vf_layout="heads", prev_k=akp, prev_v=avp)
            xp = _diff_attention_prompt(qt, k, vt, xp, w_out, sg, lam_p, batch=batch, seq=seq, lam_init=lam_init)
            q, k, v, kf, vf = _project_rows(xs, g_mix, w_in, qg, kg, bd, rope_tables=rope_s)
            xs = _diff_attention_decode(q, a_kt, a_vh, k, v, xs, w_out, sg, lam_p, layer=idx, lam_init=lam_init)
            outs["aks"].append(kf.reshape(dbatch, dseq, n_heads, 2, GROUP))
            outs["avs"].append(vf.reshape(dbatch, dseq, n_heads, LANES))
        elif kind == 1:
            w = b_w_out.shape[1]
            p = dict(w_in=b_w_in[idx].astype(BF16), b_in=row(b_b_in[idx]), conv_w=b_conv_w[idx].astype(F32),
                     conv_b=row(b_conv_b[idx]), ga_w=b_gate_a_w[idx].astype(BF16), ga_b=row(b_gate_a_b[idx]),
                     gx_w=b_gate_x_w[idx].astype(BF16), gx_b=row(b_gate_x_b[idx]), lam=row(b_lambda[idx]),
                     w_out=b_w_out[idx].astype(BF16))
            zeros = jnp.zeros((batch, SUBLANES, w), F32)
            xp, cst, hst = _rglru(xp, g_mix, p, zeros, zeros, batch=batch, seq=seq)
            outs["bcp"].append(cst[:, SUBLANES - (B_CONV_W - 1):])
            outs["bhp"].append(hst[:, 0])
            hist = jnp.pad(state_b_conv[idx].astype(F32), ((0, 0), (SUBLANES - (B_CONV_W - 1), 0), (0, 0)))
            h0 = jnp.broadcast_to(state_b_h[idx].astype(F32)[:, None, :], (dbatch, SUBLANES, w))
            xs, cst, hst = _rglru(xs, g_mix, p, hist, h0, batch=dbatch, seq=dseq)
            outs["bcs"].append(cst[:, SUBLANES - (B_CONV_W - 1):])
            outs["bhs"].append(hst[:, 0])
        else:
            w_in = c_w_in[idx].astype(BF16)
            w_out = c_w_out[idx].astype(BF16)
            qg, kg = tile_gain(c_q_norm_g[idx]), tile_gain(c_k_norm_g[idx])
            win = c_kt.shape[3]
            bias_p, bias_s = _band_bias(c_rel_bias[idx].astype(F32).T, t=att_t, n_blocks=n_band_blocks, ts=dseq,
                                        win=win)
            qt, k, vt, ckp, cvp = _project_prompt(xp, g_mix, w_in.T, col_gain(c_q_norm_g[idx]),
                                                  col_gain(c_k_norm_g[idx]), batch=batch, seq=seq, keep=keep,
                                                  prev_k=ckp, prev_v=cvp)
            xp = _band_attention_prompt(qt, k, vt, bias_p, xp, w_out, batch=batch, seq=seq)
            q, k, v, kf, vf = _project_rows(xs, g_mix, w_in, qg, kg, bd)
            xs = _band_attention_decode(q, c_kt, c_vt, k, v, bias_s, xs, w_out, layer=idx)
            outs["cks"].append(kf.reshape(dbatch, dseq, c_heads, GROUP))
            outs["cvs"].append(vf.reshape(dbatch, dseq, c_heads, GROUP))
        g_mlp = row(norm_mlp_g[layer])
        w1 = mlp_w1[layer].astype(BF16)
        w2 = mlp_w2[layer].astype(BF16)
        last = layer == depth - 1
        xp = _mlp(xp, g_mlp, w1, w2, row(norm_final_g), final_norm=last)
        xs = _mlp(xs, g_mlp, w1, w2, row(norm_final_g), final_norm=last)
    st = lambda name: jnp.stack(outs[name])
    return (xp.reshape(batch, seq, d), xs.reshape(dbatch, dseq, d),
            _position_major(akp, (n_heads, 2, GROUP)), avp.reshape(-1, batch, seq, n_heads, LANES),
            st("aks"), st("avs"),
            st("bcp"), st("bhp"), st("bcs"), st("bhs"),
            _position_major(ckp, (c_heads, GROUP)), _position_major(cvp, (c_heads, GROUP)),
            st("cks"), st("cvs"))
```

```python
import functools
import math

import jax
import jax.numpy as jnp
from jax import lax
from jax.experimental import pallas as pl
from jax.experimental.pallas import tpu as pltpu

F32 = jnp.float32
BF16 = jnp.bfloat16

EPS = 1e-6
CHUNK = 64
GROUP = 64
LANES = 128
SUBLANES = 8
ROPE_THETA = 10000.0
LOG2E = 1.0 / math.log(2.0)
Q_SCALE = GROUP ** -0.5 * LOG2E
B_C = 8.0
B_CONV_W = 4
B_BLOCKS = 4
C_LEFT_CHUNKS = 8
C_BAND_PAST = C_LEFT_CHUNKS * CHUNK
C_REL_CLIP = 128
NEG = -1e30
VMEM_LIMIT_BYTES = 56 * 1024 * 1024

ROW_TILE = 512
ATT_TILE = 256
DEC_KEY_TILE = 2048
FF_CHUNK = 512
PROJ_ROWS = 256
ONES_ROWS = 16
V_ROWS = LANES + ONES_ROWS


def _row_tile(rows, target=ROW_TILE):
    t = min(rows, target)
    assert rows % t == 0, (rows, t)
    return t


def _params(*semantics):
    return pltpu.CompilerParams(dimension_semantics=semantics, vmem_limit_bytes=VMEM_LIMIT_BYTES)


def _const_spec(shape):
    nd = len(shape)
    return pl.BlockSpec(shape, lambda *_: (0,) * nd, pipeline_mode=pl.Buffered(1))


def _rms(x, g):
    ms = jnp.mean(x * x, axis=-1, keepdims=True)
    return x * lax.rsqrt(ms + EPS) * g


def _head(h):
    return slice(h * LANES, (h + 1) * LANES)


def _proj_rows_kernel(*refs, rope):
    it = iter(refs)
    x_ref, g_ref, w_ref, qg_ref, kg_ref, bd_ref = (next(it) for _ in range(6))
    cos_ref, sin_ref = (next(it), next(it)) if rope else (None, None)
    q_ref, k_ref, v_ref, kf_ref, vf_ref = (next(it) for _ in range(5))
    x = x_ref[...]
    tm, d = x.shape
    xn = _rms(x, g_ref[...]).astype(BF16)
    qkv = jnp.dot(xn, w_ref[...], preferred_element_type=F32)
    bd = bd_ref[...]

    def group_norm(t, gain):
        sq = (t * t).astype(BF16)
        ss = jnp.concatenate([jnp.dot(sq[:, _head(j)], bd, preferred_element_type=F32)
                              for j in range(d // LANES)], axis=1)
        return t * lax.rsqrt(ss * (1.0 / GROUP) + EPS) * gain

    def rotary(t):
        cos = cos_ref[...]
        sin = sin_ref[...]
        lane = lax.broadcasted_iota(jnp.int32, (tm, LANES), 1)
        first_half = (lane % GROUP) < (GROUP // 2)
        outs = []
        for j in range(d // LANES):
            tj = t[:, _head(j)]
            partner = jnp.where(first_half, pltpu.roll(tj, LANES - GROUP // 2, 1),
                                pltpu.roll(tj, GROUP // 2, 1))
            outs.append(tj * cos + partner * sin)
        return jnp.concatenate(outs, axis=1)

    q = group_norm(qkv[:, :d], qg_ref[...])
    k = group_norm(qkv[:, d:2 * d], kg_ref[...])
    v = qkv[:, 2 * d:]
    if rope:
        q = rotary(q)
        k = rotary(k)
    q_ref[...] = (q * Q_SCALE).astype(BF16)
    k_ref[...] = k.astype(BF16)
    v_ref[...] = v.astype(BF16)
    kf_ref[...] = k
    vf_ref[...] = v


def _project_rows(x, g, w_in, q_gain, k_gain, bd, rope_tables=None):
    rows, d = x.shape
    tm = _row_tile(rows)
    row_spec = pl.BlockSpec((tm, d), lambda i: (i, 0))
    in_specs = [row_spec, _const_spec((1, d)), _const_spec((d, 3 * d)), _const_spec((1, d)),
                _const_spec((1, d)), _const_spec((LANES, LANES))]
    args = [x, g, w_in, q_gain, k_gain, bd]
    if rope_tables is not None:
        in_specs += [pl.BlockSpec((tm, LANES), lambda i: (i, 0))] * 2
        args += list(rope_tables)
    return pl.pallas_call(
        functools.partial(_proj_rows_kernel, rope=rope_tables is not None),
        grid=(rows // tm,),
        in_specs=in_specs,
        out_specs=[row_spec] * 5,
        out_shape=[jax.ShapeDtypeStruct((rows, d), BF16)] * 3 + [jax.ShapeDtypeStruct((rows, d), F32)] * 2,
        compiler_params=_params("arbitrary"),
        name="qkv_project_rows",
    )(*args)


def _proj_prompt_kernel(*refs, rope, vf_layout, n_prev, n_t, keep_t, att_t):
    it = iter(refs)
    x_ref, g_ref, wt_ref, qg_ref, kg_ref = (next(it) for _ in range(5))
    cos_ref, sin_ref = (next(it), next(it)) if rope else (None, None)
    pk_ref, pv_ref = (next(it), next(it)) if n_prev else (None, None)
    qt_ref, k_ref, vt_ref, kf_ref, vf_ref, kt_scr, vt_scr = (next(it) for _ in range(7))
    x = x_ref[...]
    tm, d = x.shape
    xn = _rms(x, g_ref[...]).astype(BF16)
    half = GROUP // 2

    def project(first_row, n_rows):
        return lax.dot_general(wt_ref[first_row:first_row + n_rows, :], xn, (((1,), (1,)), ((), ())),
                               preferred_element_type=F32)

    def norm_rope(t, gain):
        ss = jnp.sum(t * t, axis=0, keepdims=True)
        tn = t * lax.rsqrt(ss * (1.0 / GROUP) + EPS) * gain
        if rope:
            cos, sin = cos_ref[...], sin_ref[...]
            x1, x2 = tn[:half], tn[half:]
            tn = jnp.concatenate([x1 * cos - x2 * sin, x2 * cos + x1 * sin], axis=0)
        return tn

    for c0 in range(0, d, PROJ_ROWS):
        q_c, k_c = project(c0, PROJ_ROWS), project(d + c0, PROJ_ROWS)
        vt_scr[c0:c0 + PROJ_ROWS, :] = project(2 * d + c0, PROJ_ROWS)
        for gi in range(PROJ_ROWS // GROUP):
            loc = slice(gi * GROUP, (gi + 1) * GROUP)
            rows = slice(c0 + gi * GROUP, c0 + (gi + 1) * GROUP)
            qt_ref[rows, :] = (norm_rope(q_c[loc], qg_ref[rows, :]) * Q_SCALE).astype(BF16)
            kt_scr[rows, :] = norm_rope(k_c[loc], kg_ref[rows, :])
    k_ref[...] = kt_scr[...].T.astype(BF16)
    vt = vt_scr[...]
    ones = jnp.ones((ONES_ROWS, att_t), BF16)
    for c in range(tm // att_t):
        for h in range(d // LANES):
            vt_ref[c, h * V_ROWS:h * V_ROWS + LANES, :] = vt[_head(h), c * att_t:(c + 1) * att_t].astype(BF16)
            vt_ref[c, h * V_ROWS + LANES:(h + 1) * V_ROWS, :] = ones

    def write_f32():
        if n_prev:
            kf_ref[0:n_prev] = pk_ref[...]
            vf_ref[0:n_prev] = pv_ref[...]
        kf_ref[n_prev] = kt_scr[...]
        if vf_layout == "heads":
            v = vt.T
            vf_new = vf_ref.at[n_prev]
            for h in range(d // LANES):
                vf_new[pl.ds(h, tm, stride=d // LANES), :] = v[:, _head(h)]
        else:
            vf_ref[n_prev] = vt

    if keep_t < n_t:
        pl.when(pl.program_id(0) % n_t >= n_t - keep_t)(write_f32)
    else:
        write_f32()


def _project_prompt(x, g, w_in_t, q_gain, k_gain, *, batch, seq, keep, rope_tables=None, vf_layout="cols",
                    prev_k=None, prev_v=None):
    rows, d = x.shape
    n_heads = d // LANES
    tm = _row_tile(math.gcd(seq, keep))
    n_t, keep_t = seq // tm, keep // tm
    att_t = min(ATT_TILE, tm)
    n_prev = 0 if prev_k is None else prev_k.shape[0]
    n = n_prev + 1

    kept_tile = lambda i: jnp.maximum(i % n_t - (n_t - keep_t), 0)
    cols_spec = lambda m: pl.BlockSpec((m, None, d, tm), lambda i: (0, i // n_t, 0, kept_tile(i)))
    heads_spec = lambda m: pl.BlockSpec((m, tm * n_heads, LANES),
                                        lambda i: (0, (i // n_t) * keep_t + kept_tile(i), 0))
    v_f32_spec = heads_spec if vf_layout == "heads" else cols_spec
    v_f32_shape = (n, batch * keep * n_heads, LANES) if vf_layout == "heads" else (n, batch, d, keep)

    row_spec = pl.BlockSpec((tm, d), lambda i: (i, 0))
    in_specs = [row_spec, _const_spec((1, d)), _const_spec((3 * d, d)), _const_spec((d, tm)), _const_spec((d, tm))]
    args = [x, g, w_in_t, q_gain, k_gain]
    if rope_tables is not None:
        cos, sin = rope_tables
        n_rt = cos.shape[1] // tm
        in_specs += [pl.BlockSpec((GROUP // 2, tm), lambda i: (0, i % n_rt))] * 2
        args += [cos, sin]
    if n_prev:
        in_specs += [cols_spec(n_prev), v_f32_spec(n_prev)]
        args += [prev_k, prev_v]
    return pl.pallas_call(
        functools.partial(_proj_prompt_kernel, rope=rope_tables is not None, vf_layout=vf_layout, n_prev=n_prev,
                          n_t=n_t, keep_t=keep_t, att_t=att_t),
        grid=(rows // tm,),
        in_specs=in_specs,
        out_specs=[pl.BlockSpec((None, d, tm), lambda i: (i // n_t, 0, i % n_t)), row_spec,
                   pl.BlockSpec((tm // att_t, n_heads * V_ROWS, att_t), lambda i: (i, 0, 0)), cols_spec(n),
                   v_f32_spec(n)],
        out_shape=[jax.ShapeDtypeStruct((batch, d, seq), BF16), jax.ShapeDtypeStruct((rows, d), BF16),
                   jax.ShapeDtypeStruct((rows // att_t, n_heads * V_ROWS, att_t), BF16),
                   jax.ShapeDtypeStruct((n, batch, d, keep), F32), jax.ShapeDtypeStruct(v_f32_shape, F32)],
        scratch_shapes=[pltpu.VMEM((d, tm), F32), pltpu.VMEM((d, tm), F32)],
        compiler_params=_params("arbitrary"),
        name="qkv_project_prompt",
    )(*args)


def _lambda_value(lp, lam_init):
    return (jnp.exp(jnp.sum(lp[0:1] * lp[1:2], keepdims=True))
            - jnp.exp(jnp.sum(lp[2:3] * lp[3:4], keepdims=True)) + lam_init)


def _reset_state(m_scr, l_scr, acc_scr):
    m_scr[...] = jnp.full(m_scr.shape, NEG, F32)
    l_scr[...] = jnp.zeros(l_scr.shape, F32)
    acc_scr[...] = jnp.zeros(acc_scr.shape, F32)


def _stack_groups_t(qt):
    feat = lax.broadcasted_iota(jnp.int32, qt.shape, 0)
    zero = jnp.zeros_like(qt)
    low = feat < GROUP
    return jnp.concatenate([jnp.where(low, qt, zero), jnp.where(low, zero, qt)], axis=1)


def _reset_state_t(m_scr, acc_scr):
    m_scr[...] = jnp.full(m_scr.shape, NEG, F32)
    acc_scr[...] = jnp.zeros(acc_scr.shape, F32)


def _update_state_t(m_scr, acc_scr, h, s, vt):
    m = m_scr[h]
    m_new = jnp.maximum(m, jnp.max(s, axis=0, keepdims=True))
    p = jnp.exp2(s - m_new)
    acc_scr[h] = jnp.exp2(m - m_new) * acc_scr[h] + jnp.dot(vt, p.astype(BF16), preferred_element_type=F32)
    m_scr[h] = m_new


def _normalised_t(acc):
    return acc[:LANES] * (1.0 / acc[LANES:LANES + 1])


def _state_scratch_t(n_heads, t):
    return [pltpu.VMEM((n_heads, 1, 2 * t), F32), pltpu.VMEM((n_heads, V_ROWS, 2 * t), F32)]


def _stack_groups(qh):
    lane = lax.broadcasted_iota(jnp.int32, qh.shape, 1)
    zero = jnp.zeros_like(qh)
    low = lane < GROUP
    return jnp.concatenate([jnp.where(low, qh, zero), jnp.where(low, zero, qh)], axis=0)


def _dot_nt(a, b):
    return lax.dot_general(a, b, (((1,), (1,)), ((), ())), preferred_element_type=F32)


def _update_state(m_scr, l_scr, acc_scr, h, s, pv):
    m = m_scr[h]
    m_new = jnp.maximum(m, jnp.max(s, axis=1, keepdims=True))
    alpha = jnp.exp2(m - m_new)
    p = jnp.exp2(s - m_new)
    l_scr[h] = alpha * l_scr[h] + jnp.sum(p, axis=1, keepdims=True)
    acc_scr[h] = alpha * acc_scr[h] + pv(p.astype(BF16))
    m_scr[h] = m_new


def _state_scratch(n_heads, t):
    return [pltpu.VMEM((n_heads, 2 * t, 1), F32), pltpu.VMEM((n_heads, 2 * t, 1), F32),
            pltpu.VMEM((n_heads, 2 * t, LANES), F32)]


def _diff_combine(on, t, lam, sub_gain, lam_init):
    o = on[:t] - lam * on[t:]
    return _rms(o, sub_gain) * (1.0 - lam_init)


def _select_groups(on, t):
    lane = lax.broadcasted_iota(jnp.int32, (t, LANES), 1)
    return jnp.where(lane < GROUP, on[:t], on[t:])


def _diff_prompt_kernel(qt_ref, k_ref, vt_ref, x_ref, wo_ref, sg_ref, lp_ref, o_ref,
                        qs_scr, m_scr, acc_scr, s_scr, mask_scr, obuf, *, lam_init):
    i = pl.program_id(1)
    d, t = qt_ref.shape
    n_heads = d // LANES

    @pl.when((pl.program_id(0) == 0) & (i == 0))
    def _():
        key = lax.broadcasted_iota(jnp.int32, (t, 2 * t), 0)
        qry = lax.broadcasted_iota(jnp.int32, (t, 2 * t), 1) % t
        mask_scr[...] = jnp.where((key // CHUNK) <= (qry // CHUNK), 0.0, NEG)

    for h in range(n_heads):
        qs_scr[h] = _stack_groups_t(qt_ref[_head(h), :])
    _reset_state_t(m_scr, acc_scr)

    def scores(j, slot, diagonal=False):
        rows = pl.ds(pl.multiple_of(j * t, t), t)
        for h in range(n_heads):
            s = jnp.dot(k_ref[rows, _head(h)], qs_scr[h], preferred_element_type=F32)
            s_scr[slot, h] = s + mask_scr[...] if diagonal else s

    def consume(j, slot):
        for h in range(n_heads):
            _update_state_t(m_scr, acc_scr, h, s_scr[slot, h], vt_ref[j, h * V_ROWS:(h + 1) * V_ROWS, :])

    pairs = jnp.maximum(i - 1, 0) // 2
    pl.when(i > 0)(lambda: scores(0, 0))

    def body(p, c):
        j = 2 * p
        scores(j + 1, 1)
        consume(j, 0)
        scores(j + 2, 0)
        consume(j + 1, 1)
        return c

    lax.fori_loop(0, pairs, body, 0)
    left = i - 2 * pairs

    @pl.when(left == 0)
    def _():
        scores(i, 0, True)
        consume(i, 0)

    @pl.when(left == 1)
    def _():
        scores(i, 1, True)
        consume(i - 1, 0)
        consume(i, 1)

    @pl.when(left == 2)
    def _():
        scores(i - 1, 1)
        consume(i - 2, 0)
        scores(i, 0, True)
        consume(i - 1, 1)
        consume(i, 0)

    lam = _lambda_value(lp_ref[...], lam_init)
    for h in range(n_heads):
        on = _normalised_t(acc_scr[h])
        o = on[:, :t] - lam * on[:, t:]
        o = o * lax.rsqrt(jnp.mean(o * o, axis=0, keepdims=True) + EPS)
        obuf[:, _head(h)] = (o.T * sg_ref[:, _head(h)] * (1.0 - lam_init)).astype(BF16)
    o_ref[...] = x_ref[...] + jnp.dot(obuf[...], wo_ref[...], preferred_element_type=F32)


def _diff_attention_prompt(qt, k, vt, x, w_out, sub_gain, lam_p, *, batch, seq, lam_init):
    rows, d = x.shape
    t = vt.shape[2]
    nq = seq // t
    n_heads = d // LANES
    row_spec = pl.BlockSpec((t, d), lambda b, i: (b * nq + i, 0))
    return pl.pallas_call(
        functools.partial(_diff_prompt_kernel, lam_init=lam_init),
        grid=(batch, nq),
        in_specs=[pl.BlockSpec((None, d, t), lambda b, i: (b, 0, i)),
                  pl.BlockSpec((seq, d), lambda b, i: (b, 0)),
                  pl.BlockSpec((nq, n_heads * V_ROWS, t), lambda b, i: (b, 0, 0)),
                  row_spec, _const_spec((d, d)), _const_spec((1, d)), _const_spec(lam_p.shape)],
        out_specs=row_spec,
        out_shape=jax.ShapeDtypeStruct((rows, d), F32),
        scratch_shapes=[pltpu.VMEM((n_heads, LANES, 2 * t), BF16)] + _state_scratch_t(n_heads, t)
        + [pltpu.VMEM((2, n_heads, t, 2 * t), F32), pltpu.VMEM((t, 2 * t), F32), pltpu.VMEM((t, d), BF16)],
        compiler_params=_params("arbitrary", "arbitrary"),
        name="diff_attention_prompt",
    )(qt, k, vt, x, w_out, sub_gain, lam_p)


def _diff_decode_kernel(q_ref, kc_ref, vc_ref, kn_ref, vn_ref, x_ref, wo_ref, sg_ref, lp_ref, o_ref,
                        m_scr, l_scr, acc_scr, obuf, *, lam_init):
    j = pl.program_id(1)
    t, d = q_ref.shape
    n_heads = d // LANES
    tk = kc_ref.shape[1]

    @pl.when(j == 0)
    def _():
        _reset_state(m_scr, l_scr, acc_scr)

    for h in range(n_heads):
        qs = _stack_groups(q_ref[:, _head(h)])
        s = jnp.dot(qs, kc_ref[_head(h), :].astype(BF16), preferred_element_type=F32)
        vh = vc_ref[pl.ds(h, tk, stride=n_heads), :].astype(BF16)
        _update_state(m_scr, l_scr, acc_scr, h, s, lambda p: jnp.dot(p, vh, preferred_element_type=F32))

    @pl.when(j == pl.num_programs(1) - 1)
    def _():
        lam = _lambda_value(lp_ref[...], lam_init)
        for h in range(n_heads):
            qs = _stack_groups(q_ref[:, _head(h)])
            vh = vn_ref[:, _head(h)]
            _update_state(m_scr, l_scr, acc_scr, h, _dot_nt(qs, kn_ref[:, _head(h)]),
                          lambda p: jnp.dot(p, vh, preferred_element_type=F32))
            on = acc_scr[h] / l_scr[h]
            obuf[:, _head(h)] = _diff_combine(on, t, lam, sg_ref[:, _head(h)], lam_init).astype(BF16)
        o_ref[...] = x_ref[...] + jnp.dot(obuf[...], wo_ref[...], preferred_element_type=F32)


def _diff_attention_decode(q, cache_kt, cache_vh, k_new, v_new, x, w_out, sub_gain, lam_p, *, layer, lam_init):
    _, batch, d, past = cache_kt.shape
    rows = x.shape[0]
    t = rows // batch
    tk = min(DEC_KEY_TILE, past)
    n_heads = d // LANES
    row_spec = pl.BlockSpec((t, d), lambda b, j: (b, 0))
    kt_spec = pl.BlockSpec((None, None, d, tk), lambda b, j: (layer, b, 0, j))
    v_spec = pl.BlockSpec((None, None, tk * n_heads, LANES), lambda b, j: (layer, b, j, 0))
    return pl.pallas_call(
        functools.partial(_diff_decode_kernel, lam_init=lam_init),
        grid=(batch, past // tk),
        in_specs=[row_spec, kt_spec, v_spec, row_spec, row_spec, row_spec, _const_spec((d, d)),
                  _const_spec((1, d)), _const_spec(lam_p.shape)],
        out_specs=row_spec,
        out_shape=jax.ShapeDtypeStruct((rows, d), F32),
        scratch_shapes=_state_scratch(n_heads, t) + [pltpu.VMEM((t, d), BF16)],
        compiler_params=_params("arbitrary", "arbitrary"),
        name="diff_attention_decode",
    )(q, cache_kt, cache_vh, k_new, v_new, x, w_out, sub_gain, lam_p)


def _gather_rows(tbl, idx_of_col, width):
    n_pad = tbl.shape[1]
    t_idx = lax.broadcasted_iota(jnp.int32, (n_pad, width), 0)
    v_idx = lax.broadcasted_iota(jnp.int32, (n_pad, width), 1)
    onehot = jnp.where(t_idx == idx_of_col(v_idx), 1.0, 0.0).astype(BF16)
    hi = tbl.astype(BF16)
    r1 = tbl - hi.astype(F32)
    mid = r1.astype(BF16)
    lo = (r1 - mid.astype(F32)).astype(BF16)
    dot = lambda a: jnp.dot(a, onehot, preferred_element_type=F32)
    return dot(hi) + dot(mid) + dot(lo)


def _rel_index(rel):
    return jnp.clip(rel, -C_REL_CLIP, C_REL_CLIP) + C_REL_CLIP


def _toeplitz(vec, rows, cols, origin):
    width = vec.shape[1]
    xb = jnp.broadcast_to(vec, (rows, width))
    return pltpu.roll(xb, (width - origin) % width, 1, stride=1, stride_axis=0)[:, :cols]


def _bias_kernel(tbl_ref, bp_ref, bs_ref, *, t, n_blocks, ts, win):
    tbl = tbl_ref[...] * LOG2E
    heads = tbl.shape[0]
    key = lax.broadcasted_iota(jnp.int32, (t, t), 0)
    qry = lax.broadcasted_iota(jnp.int32, (t, t), 1)
    for jb in range(n_blocks):
        off = (n_blocks - 1 - jb) * t
        g = _gather_rows(tbl, lambda v: _rel_index(v - (t - 1) + off), 2 * t)
        key_chunk = (jb * t + key) // CHUNK - (n_blocks - 1) * t // CHUNK
        q_chunk = qry // CHUNK
        visible = (key_chunk <= q_chunk) & (key_chunk >= q_chunk - C_LEFT_CHUNKS)
        for h in range(heads):
            tile = _toeplitz(g[h:h + 1, :], t, t, t - 1)
            bp_ref[jb, h // 2, :, (h % 2) * t:(h % 2 + 1) * t] = jnp.where(visible, tile, NEG)
    width = bs_ref.shape[-1]
    wide = 1 << (width + ts).bit_length()
    g = _gather_rows(tbl, lambda v: _rel_index(win + ts - 1 - v), wide)
    for h in range(heads):
        tile = _toeplitz(g[h:h + 1, :], ts, width, ts - 1)
        bs_ref[h // 2, (h % 2) * ts:(h % 2 + 1) * ts, :] = tile


def _band_bias(table_t, *, t, n_blocks, ts, win):
    heads, n = table_t.shape
    n_pad = -(-n // LANES) * LANES
    tbl = jnp.pad(table_t, ((0, 0), (0, n_pad - n)))
    return pl.pallas_call(
        functools.partial(_bias_kernel, t=t, n_blocks=n_blocks, ts=ts, win=win),
        out_shape=[jax.ShapeDtypeStruct((n_blocks, heads // 2, t, 2 * t), F32),
                   jax.ShapeDtypeStruct((heads // 2, 2 * ts, win + ts), F32)],
        compiler_params=pltpu.CompilerParams(vmem_limit_bytes=VMEM_LIMIT_BYTES),
        name="band_bias",
    )(tbl)


def _band_prompt_kernel(qt_ref, k_ref, vt_ref, b_ref, x_ref, wo_ref, o_ref, qs_scr, m_scr, acc_scr, s_scr, obuf):
    i = pl.program_id(1)
    d, t = qt_ref.shape
    n_heads = d // LANES
    n_blocks = b_ref.shape[0]
    first = jnp.maximum(n_blocks - 1 - i, 0)
    for h in range(n_heads):
        qs_scr[h] = _stack_groups_t(qt_ref[_head(h), :])
    _reset_state_t(m_scr, acc_scr)

    def scores(jb):
        rows = pl.ds(pl.multiple_of((i - (n_blocks - 1) + jb) * t, t), t)
        for h in range(n_heads):
            s_scr[jb % 2, h] = (jnp.dot(k_ref[rows, _head(h)], qs_scr[h], preferred_element_type=F32)
                                + b_ref[jb, h])

    def consume(jb):
        j = i - (n_blocks - 1) + jb
        for h in range(n_heads):
            _update_state_t(m_scr, acc_scr, h, s_scr[jb % 2, h], vt_ref[j, h * V_ROWS:(h + 1) * V_ROWS, :])

    def window(f):
        scores(f)
        for jb in range(f, n_blocks):
            if jb + 1 < n_blocks:
                scores(jb + 1)
            consume(jb)

    for f in range(n_blocks):
        pl.when(first == f)(functools.partial(window, f))
    feat = lax.broadcasted_iota(jnp.int32, (LANES, t), 0)
    for h in range(n_heads):
        on = _normalised_t(acc_scr[h])
        obuf[:, _head(h)] = jnp.where(feat < GROUP, on[:, :t], on[:, t:]).T.astype(BF16)
    o_ref[...] = x_ref[...] + jnp.dot(obuf[...], wo_ref[...], preferred_element_type=F32)


def _band_attention_prompt(qt, k, vt, bias, x, w_out, *, batch, seq):
    rows, d = x.shape
    t = vt.shape[2]
    nq = seq // t
    n_heads = d // LANES
    row_spec = pl.BlockSpec((t, d), lambda b, i: (b * nq + i, 0))
    return pl.pallas_call(
        _band_prompt_kernel,
        grid=(batch, nq),
        in_specs=[pl.BlockSpec((None, d, t), lambda b, i: (b, 0, i)),
                  pl.BlockSpec((seq, d), lambda b, i: (b, 0)),
                  pl.BlockSpec((nq, n_heads * V_ROWS, t), lambda b, i: (b, 0, 0)),
                  _const_spec(bias.shape), row_spec, _const_spec((d, d))],
        out_specs=row_spec,
        out_shape=jax.ShapeDtypeStruct((rows, d), F32),
        scratch_shapes=[pltpu.VMEM((n_heads, LANES, 2 * t), BF16)] + _state_scratch_t(n_heads, t)
        + [pltpu.VMEM((2, n_heads, t, 2 * t), F32), pltpu.VMEM((t, d), BF16)],
        compiler_params=_params("arbitrary", "arbitrary"),
        name="band_attention_prompt",
    )(qt, k, vt, bias, x, w_out)


def _band_decode_kernel(q_ref, kc_ref, vc_ref, kn_ref, vn_ref, b_ref, x_ref, wo_ref, o_ref, obuf):
    t, d = q_ref.shape
    win = kc_ref.shape[1]
    for h in range(d // LANES):
        qs = _stack_groups(q_ref[:, _head(h)])
        bias = b_ref[h]
        s_c = jnp.dot(qs, kc_ref[_head(h), :].astype(BF16), preferred_element_type=F32) + bias[:, :win]
        s_n = _dot_nt(qs, kn_ref[:, _head(h)]) + bias[:, win:]
        m = jnp.maximum(jnp.max(s_c, axis=1, keepdims=True), jnp.max(s_n, axis=1, keepdims=True))
        p_c = jnp.exp2(s_c - m)
        p_n = jnp.exp2(s_n - m)
        l = jnp.sum(p_c, axis=1, keepdims=True) + jnp.sum(p_n, axis=1, keepdims=True)
        acc = (_dot_nt(p_c.astype(BF16), vc_ref[_head(h), :].astype(BF16))
               + jnp.dot(p_n.astype(BF16), vn_ref[:, _head(h)], preferred_element_type=F32))
        obuf[:, _head(h)] = _select_groups(acc / l, t).astype(BF16)
    o_ref[...] = x_ref[...] + jnp.dot(obuf[...], wo_ref[...], preferred_element_type=F32)


def _band_attention_decode(q, cache_kt, cache_vt, k_new, v_new, bias, x, w_out, *, layer):
    _, batch, d, win = cache_kt.shape
    rows = x.shape[0]
    t = rows // batch
    row_spec = pl.BlockSpec((t, d), lambda b: (b, 0))
    cache_spec = pl.BlockSpec((None, None, d, win), lambda b: (layer, b, 0, 0))
    return pl.pallas_call(
        _band_decode_kernel,
        grid=(batch,),
        in_specs=[row_spec, cache_spec, cache_spec, row_spec, row_spec, _const_spec(bias.shape), row_spec,
                  _const_spec((d, d))],
        out_specs=row_spec,
        out_shape=jax.ShapeDtypeStruct((rows, d), F32),
        scratch_shapes=[pltpu.VMEM((t, d), BF16)],
        compiler_params=_params("arbitrary"),
        name="band_attention_decode",
    )(q, cache_kt, cache_vt, k_new, v_new, bias, x, w_out)


def _gelu_tanh(x):
    c = -2.0 * math.sqrt(2.0 / math.pi) * LOG2E
    return x / (1.0 + jnp.exp2(x * (c + (0.044715 * c) * (x * x))))


def _rglru_kernel(x_ref, g_ref, win_ref, bin_ref, cw_ref, cb_ref, gaw_ref, gab_ref, gxw_ref, gxb_ref,
                  lam_ref, wout_ref, hist_ref, h0_ref, o_ref, cst_ref, hst_ref, ubuf, hcar, a_scr, b_scr):
    tm, w = a_scr.shape[0] - SUBLANES, a_scr.shape[1]
    pad = SUBLANES

    @pl.when(pl.program_id(1) == 0)
    def _():
        ubuf[0:pad, :] = hist_ref[...]
        hcar[...] = h0_ref[...]
        a_scr[0:pad, :] = jnp.zeros((pad, w), F32)
        b_scr[0:pad, :] = jnp.zeros((pad, w), F32)

    x = x_ref[...]
    xn = _rms(x, g_ref[...]).astype(BF16)
    gu = jnp.dot(xn, win_ref[...], preferred_element_type=F32) + bin_ref[...]
    gate = _gelu_tanh(gu[:, :w])
    u = gu[:, w:]
    ubuf[pad:pad + tm, :] = u
    cw = cw_ref[...]
    xc = cb_ref[...] + cw[3:4] * u
    for j in range(B_CONV_W - 1):
        xc = xc + cw[j:j + 1] * ubuf[pad - 3 + j:pad - 3 + j + tm, :]

    xcb = xc.astype(BF16)
    bw = w // B_BLOCKS
    r_parts, i_parts = [], []
    for n in range(B_BLOCKS):
        blk = xcb[:, n * bw:(n + 1) * bw]
        r_parts.append(jnp.dot(blk, gaw_ref[n], preferred_element_type=F32))
        i_parts.append(jnp.dot(blk, gxw_ref[n], preferred_element_type=F32))
    r = jax.nn.sigmoid(jnp.concatenate(r_parts, axis=1) + gab_ref[...])
    ig = jax.nn.sigmoid(jnp.concatenate(i_parts, axis=1) + gxb_ref[...])
    z = -lam_ref[...]
    softplus = jnp.maximum(z, 0.0) + jnp.log1p(jnp.exp(-jnp.abs(z)))
    a = jnp.exp(-B_C * r * softplus)
    b = jnp.sqrt(1.0 - a * a) * (ig * xc)

    sub = lax.broadcasted_iota(jnp.int32, (tm, w), 0) % SUBLANES
    for s in (1, 2, 4):
        a_scr[pad:pad + tm, :] = a
        b_scr[pad:pad + tm, :] = b
        a_prev = a_scr[pad - s:pad - s + tm, :]
        b_prev = b_scr[pad - s:pad - s + tm, :]
        inside = sub >= s
        b = jnp.where(inside, a * b_prev + b, b)
        a = jnp.where(inside, a * a_prev, a)
    h_prev = hcar[...]
    for gidx in range(tm // SUBLANES):
        rows = slice(gidx * SUBLANES, (gidx + 1) * SUBLANES)
        hg = a[rows] * h_prev + b[rows]
        b_scr[pad + gidx * SUBLANES:pad + (gidx + 1) * SUBLANES, :] = hg
        h_prev = jnp.broadcast_to(hg[SUBLANES - 1:SUBLANES, :], (SUBLANES, w))
    hcar[...] = h_prev
    hs = b_scr[pad:pad + tm, :]

    y = jnp.dot((hs * gate).astype(BF16), wout_ref[...], preferred_element_type=F32)
    o_ref[...] = x + y
    ubuf[0:pad, :] = u[tm - pad:tm, :]
    cst_ref[...] = u[tm - pad:tm, :]
    hst_ref[...] = h_prev


def _rglru(x, g, p, hist, h0, *, batch, seq):
    rows, d = x.shape
    w = p["w_out"].shape[0]
    tm = _row_tile(seq)
    nt = seq // tm
    row_spec = pl.BlockSpec((tm, d), lambda b, t: (b * nt + t, 0))
    state_spec = pl.BlockSpec((None, SUBLANES, w), lambda b, t: (b, 0, 0))
    consts = [g, p["w_in"], p["b_in"], p["conv_w"], p["conv_b"], p["ga_w"], p["ga_b"], p["gx_w"], p["gx_b"],
              p["lam"], p["w_out"]]
    return pl.pallas_call(
        _rglru_kernel,
        grid=(batch, nt),
        in_specs=[row_spec] + [_const_spec(c.shape) for c in consts] + [state_spec, state_spec],
        out_specs=[row_spec, state_spec, state_spec],
        out_shape=[jax.ShapeDtypeStruct((rows, d), F32), jax.ShapeDtypeStruct((batch, SUBLANES, w), F32),
                   jax.ShapeDtypeStruct((batch, SUBLANES, w), F32)],
        scratch_shapes=[pltpu.VMEM((tm + SUBLANES, w), F32), pltpu.VMEM((SUBLANES, w), F32),
                        pltpu.VMEM((tm + SUBLANES, w), F32), pltpu.VMEM((tm + SUBLANES, w), F32)],
        compiler_params=_params("arbitrary", "arbitrary"),
        name="rglru_block",
    )(x, *consts, hist, h0)


def _mlp_kernel(x_ref, g_ref, w1_ref, w2_ref, gf_ref, o_ref, *, final_norm):
    x = x_ref[...]
    xn = _rms(x, g_ref[...]).astype(BF16)
    acc = x
    for c in range(w1_ref.shape[1] // FF_CHUNK):
        cols = slice(c * FF_CHUNK, (c + 1) * FF_CHUNK)
        h = jnp.dot(xn, w1_ref[:, cols], preferred_element_type=F32)
        h = jnp.square(jnp.maximum(h, 0.0)).astype(BF16)
        acc = acc + jnp.dot(h, w2_ref[cols, :], preferred_element_type=F32)
    if final_norm:
        acc = _rms(acc, gf_ref[...])
    o_ref[...] = acc


def _mlp(x, g, w1, w2, g_final, *, final_norm):
    rows, d = x.shape
    tm = _row_tile(rows)
    row_spec = pl.BlockSpec((tm, d), lambda i: (i, 0))
    return pl.pallas_call(
        functools.partial(_mlp_kernel, final_norm=final_norm),
        grid=(rows // tm,),
        in_specs=[row_spec, _const_spec((1, d)), _const_spec(w1.shape), _const_spec(w2.shape),
                  _const_spec((1, d))],
        out_specs=row_spec,
        out_shape=jax.ShapeDtypeStruct((rows, d), F32),
        compiler_params=_params("arbitrary"),
        name="mlp",
    )(x, g, w1, w2, g_final)


def _rope_angles(pos):
    half = GROUP // 2
    inv = ROPE_THETA ** (-jnp.arange(half, dtype=F32) / half)
    return pos.astype(F32)[:, None] * inv[None, :]


def _rope_tables_rows(pos):
    ang = _rope_angles(pos)
    cos = jnp.tile(jnp.cos(ang), (1, LANES // ang.shape[1]))
    sin = jnp.tile(jnp.concatenate([-jnp.sin(ang), jnp.sin(ang)], axis=1), (1, LANES // GROUP))
    return cos, sin


def _rope_tables_cols(pos):
    ang = _rope_angles(pos).T
    return jnp.cos(ang), jnp.sin(ang)


def _feature_major(cache):
    n, batch, pos = cache.shape[:3]
    nd = cache.ndim
    return jnp.transpose(cache, (0, 1) + tuple(range(3, nd)) + (2,)).reshape(n, batch, -1, pos)


def _position_major(x, feature_dims):
    n, batch, _, pos = x.shape
    nf = len(feature_dims)
    return jnp.transpose(x.reshape(n, batch, *feature_dims, pos), (0, 1, 2 + nf) + tuple(range(2, 2 + nf)))


def kernel(x_prompt, x_sample, cache_a_k, cache_a_v, state_b_conv, state_b_h, cache_c_k, cache_c_v, norm_mix_g, norm_mlp_g, norm_final_g, a_w_in, a_q_norm_g, a_k_norm_g, a_lambda, a_subln_g, a_w_out, b_w_in, b_b_in, b_conv_w, b_conv_b, b_gate_a_w, b_gate_a_b, b_gate_x_w, b_gate_x_b, b_lambda, b_w_out, c_w_in, c_q_norm_g, c_k_norm_g, c_rel_bias, c_w_out, mlp_w1, mlp_w2):
    batch, seq, d = x_prompt.shape
    dbatch, dseq, _ = x_sample.shape
    depth = norm_mix_g.shape[0]
    past = cache_a_k.shape[2]
    n_heads = d // LANES
    xp = x_prompt.reshape(batch * seq, d)
    xs = x_sample.reshape(dbatch * dseq, d)
    row = lambda v: v.reshape(1, -1).astype(F32)
    tile_gain = lambda v: jnp.tile(v.astype(F32), d // v.shape[0]).reshape(1, d)
    grp = jnp.arange(LANES) // GROUP
    bd = (grp[:, None] == grp[None, :]).astype(BF16)
    col_gain = lambda v: jnp.broadcast_to(tile_gain(v).reshape(d, 1), (d, min(ROW_TILE, seq)))
    rope_p = _rope_tables_cols(jnp.arange(seq))
    rope_s = tuple(jnp.tile(t, (dbatch, 1)) for t in _rope_tables_rows(past + jnp.arange(dseq)))
    keep = min(C_BAND_PAST, seq)
    att_t = min(ATT_TILE, seq)
    n_band_blocks = C_BAND_PAST // att_t + 1
    a_kt = _feature_major(cache_a_k)
    a_vh = cache_a_v.reshape(cache_a_v.shape[0], dbatch, past * n_heads, LANES)
    c_kt, c_vt = _feature_major(cache_c_k), _feature_major(cache_c_v)
    c_heads = c_rel_bias.shape[2]

    outs = {name: [] for name in ("aks", "avs", "bcp", "bhp", "bcs", "bhs", "cks", "cvs")}
    akp = avp = ckp = cvp = None
    for layer in range(depth):
        kind, idx = layer % 3, layer // 3
        g_mix = row(norm_mix_g[layer])
        if kind == 0:
            lam_init = 0.8 - 0.6 * math.exp(-0.3 * layer)
            w_in = a_w_in[idx].astype(BF16)
            w_out = a_w_out[idx].astype(BF16)
            qg, kg, sg = tile_gain(a_q_norm_g[idx]), tile_gain(a_k_norm_g[idx]), tile_gain(a_subln_g[idx])
            lam_p = a_lambda[idx].astype(F32)
            qt, k, vt, akp, avp = _project_prompt(xp, g_mix, w_in.T, col_gain(a_q_norm_g[idx]),
                                                  col_gain(a_k_norm_g[idx]), batch=batch, seq=seq, keep=seq,
                                                  rope_tables=rope_p, vf_layout="heads", prev_k=akp, prev_v=avp)
            xp = _diff_attention_prompt(qt, k, vt, xp, w_out, sg, lam_p, batch=batch, seq=seq, lam_init=lam_init)
            q, k, v, kf, vf = _project_rows(xs, g_mix, w_in, qg, kg, bd, rope_tables=rope_s)
            xs = _diff_attention_decode(q, a_kt, a_vh, k, v, xs, w_out, sg, lam_p, layer=idx, lam_init=lam_init)
            outs["aks"].append(kf.reshape(dbatch, dseq, n_heads, 2, GROUP))
            outs["avs"].append(vf.reshape(dbatch, dseq, n_heads, LANES))
        elif kind == 1:
            w = b_w_out.shape[1]
            p = dict(w_in=b_w_in[idx].astype(BF16), b_in=row(b_b_in[idx]), conv_w=b_conv_w[idx].astype(F32),
                     conv_b=row(b_conv_b[idx]), ga_w=b_gate_a_w[idx].astype(BF16), ga_b=row(b_gate_a_b[idx]),
                     gx_w=b_gate_x_w[idx].astype(BF16), gx_b=row(b_gate_x_b[idx]), lam=row(b_lambda[idx]),
                     w_out=b_w_out[idx].astype(BF16))
            zeros = jnp.zeros((batch, SUBLANES, w), F32)
            xp, cst, hst = _rglru(xp, g_mix, p, zeros, zeros, batch=batch, seq=seq)
            outs["bcp"].append(cst[:, SUBLANES - (B_CONV_W - 1):])
            outs["bhp"].append(hst[:, 0])
            hist = jnp.pad(state_b_conv[idx].astype(F32), ((0, 0), (SUBLANES - (B_CONV_W - 1), 0), (0, 0)))
            h0 = jnp.broadcast_to(state_b_h[idx].astype(F32)[:, None, :], (dbatch, SUBLANES, w))
            xs, cst, hst = _rglru(xs, g_mix, p, hist, h0, batch=dbatch, seq=dseq)
            outs["bcs"].append(cst[:, SUBLANES - (B_CONV_W - 1):])
            outs["bhs"].append(hst[:, 0])
        else:
            w_in = c_w_in[idx].astype(BF16)
            w_out = c_w_out[idx].astype(BF16)
            qg, kg = tile_gain(c_q_norm_g[idx]), tile_gain(c_k_norm_g[idx])
            win = c_kt.shape[3]
            bias_p, bias_s = _band_bias(c_rel_bias[idx].astype(F32).T, t=att_t, n_blocks=n_band_blocks, ts=dseq,
                                        win=win)
            qt, k, vt, ckp, cvp = _project_prompt(xp, g_mix, w_in.T, col_gain(c_q_norm_g[idx]),
                                                  col_gain(c_k_norm_g[idx]), batch=batch, seq=seq, keep=keep,
                                                  prev_k=ckp, prev_v=cvp)
            xp = _band_attention_prompt(qt, k, vt, bias_p, xp, w_out, batch=batch, seq=seq)
            q, k, v, kf, vf = _project_rows(xs, g_mix, w_in, qg, kg, bd)
            xs = _band_attention_decode(q, c_kt, c_vt, k, v, bias_s, xs, w_out, layer=idx)
            outs["cks"].append(kf.reshape(dbatch, dseq, c_heads, GROUP))
            outs["cvs"].append(vf.reshape(dbatch, dseq, c_heads, GROUP))
        g_mlp = row(norm_mlp_g[layer])
        w1 = mlp_w1[layer].astype(BF16)
        w2 = mlp_w2[layer].astype(BF16)
        last = layer == depth - 1
        xp = _mlp(xp, g_mlp, w1, w2, row(norm_final_g), final_norm=last)
        xs = _mlp(xs, g_mlp, w1, w2, row(norm_final_g), final_norm=last)
    st = lambda name: jnp.stack(outs[name])
    return (xp.reshape(batch, seq, d), xs.reshape(dbatch, dseq, d),
            _position_major(akp, (n_heads, 2, GROUP)), avp.reshape(-1, batch, seq, n_heads, LANES),
            st("aks"), st("avs"),
            st("bcp"), st("bhp"), st("bcs"), st("bhs"),
            _position_major(ckp, (c_heads, GROUP)), _position_major(cvp, (c_heads, GROUP)),
            st("cks"), st("cvs"))
```

```python
import functools
import math

import jax
import jax.numpy as jnp
from jax import lax
from jax.experimental import pallas as pl
from jax.experimental.pallas import tpu as pltpu

F32 = jnp.float32
BF16 = jnp.bfloat16

EPS = 1e-6
CHUNK = 64
GROUP = 64
LANES = 128
SUBLANES = 8
ROPE_THETA = 10000.0
LOG2E = 1.0 / math.log(2.0)
Q_SCALE = GROUP ** -0.5 * LOG2E
B_C = 8.0
B_CONV_W = 4
B_BLOCKS = 4
C_LEFT_CHUNKS = 8
C_BAND_PAST = C_LEFT_CHUNKS * CHUNK
C_REL_CLIP = 128
NEG = -1e30
VMEM_LIMIT_BYTES = 56 * 1024 * 1024

ROW_TILE = 512
MLP_TILE = 1024
ATT_TILE = 256
DEC_KEY_TILE = 2048
FF_CHUNK = 512
PROJ_ROWS = 256
ONES_ROWS = 16
V_ROWS = LANES + ONES_ROWS


def _row_tile(rows, target=ROW_TILE):
    t = min(rows, target)
    assert rows % t == 0, (rows, t)
    return t


def _params(*semantics):
    return pltpu.CompilerParams(dimension_semantics=semantics, vmem_limit_bytes=VMEM_LIMIT_BYTES)


def _const_spec(shape):
    nd = len(shape)
    return pl.BlockSpec(shape, lambda *_: (0,) * nd, pipeline_mode=pl.Buffered(1))


def _rms(x, g):
    ms = jnp.mean(x * x, axis=-1, keepdims=True)
    return x * lax.rsqrt(ms + EPS) * g


def _head(h):
    return slice(h * LANES, (h + 1) * LANES)


def _proj_rows_kernel(*refs, rope):
    it = iter(refs)
    x_ref, g_ref, w_ref, qg_ref, kg_ref, bd_ref = (next(it) for _ in range(6))
    cos_ref, sin_ref = (next(it), next(it)) if rope else (None, None)
    q_ref, k_ref, v_ref, kf_ref, vf_ref = (next(it) for _ in range(5))
    x = x_ref[...]
    tm, d = x.shape
    xn = _rms(x, g_ref[...]).astype(BF16)
    qkv = jnp.dot(xn, w_ref[...], preferred_element_type=F32)
    bd = bd_ref[...]

    def group_norm(t, gain):
        sq = (t * t).astype(BF16)
        ss = jnp.concatenate([jnp.dot(sq[:, _head(j)], bd, preferred_element_type=F32)
                              for j in range(d // LANES)], axis=1)
        return t * lax.rsqrt(ss * (1.0 / GROUP) + EPS) * gain

    def rotary(t):
        cos = cos_ref[...]
        sin = sin_ref[...]
        lane = lax.broadcasted_iota(jnp.int32, (tm, LANES), 1)
        first_half = (lane % GROUP) < (GROUP // 2)
        outs = []
        for j in range(d // LANES):
            tj = t[:, _head(j)]
            partner = jnp.where(first_half, pltpu.roll(tj, LANES - GROUP // 2, 1),
                                pltpu.roll(tj, GROUP // 2, 1))
            outs.append(tj * cos + partner * sin)
        return jnp.concatenate(outs, axis=1)

    q = group_norm(qkv[:, :d], qg_ref[...])
    k = group_norm(qkv[:, d:2 * d], kg_ref[...])
    v = qkv[:, 2 * d:]
    if rope:
        q = rotary(q)
        k = rotary(k)
    q_ref[...] = (q * Q_SCALE).astype(BF16)
    k_ref[...] = k.astype(BF16)
    v_ref[...] = v.astype(BF16)
    kf_ref[...] = k
    vf_ref[...] = v


def _project_rows(x, g, w_in, q_gain, k_gain, bd, rope_tables=None):
    rows, d = x.shape
    tm = _row_tile(rows)
    row_spec = pl.BlockSpec((tm, d), lambda i: (i, 0))
    in_specs = [row_spec, _const_spec((1, d)), _const_spec((d, 3 * d)), _const_spec((1, d)),
                _const_spec((1, d)), _const_spec((LANES, LANES))]
    args = [x, g, w_in, q_gain, k_gain, bd]
    if rope_tables is not None:
        in_specs += [pl.BlockSpec((tm, LANES), lambda i: (i, 0))] * 2
        args += list(rope_tables)
    return pl.pallas_call(
        functools.partial(_proj_rows_kernel, rope=rope_tables is not None),
        grid=(rows // tm,),
        in_specs=in_specs,
        out_specs=[row_spec] * 5,
        out_shape=[jax.ShapeDtypeStruct((rows, d), BF16)] * 3 + [jax.ShapeDtypeStruct((rows, d), F32)] * 2,
        compiler_params=_params("arbitrary"),
        name="qkv_project_rows",
    )(*args)


def _proj_prompt_kernel(*refs, rope, vf_layout, n_prev, n_t, keep_t, att_t):
    it = iter(refs)
    x_ref, g_ref, wt_ref, qg_ref, kg_ref = (next(it) for _ in range(5))
    cos_ref, sin_ref = (next(it), next(it)) if rope else (None, None)
    pk_ref, pv_ref = (next(it), next(it)) if n_prev else (None, None)
    qt_ref, k_ref, vt_ref, kf_ref, vf_ref, kt_scr, vt_scr = (next(it) for _ in range(7))
    x = x_ref[...]
    tm, d = x.shape
    xn = _rms(x, g_ref[...]).astype(BF16)
    half = GROUP // 2

    def project(first_row, n_rows):
        return lax.dot_general(wt_ref[first_row:first_row + n_rows, :], xn, (((1,), (1,)), ((), ())),
                               preferred_element_type=F32)

    def norm_rope(t, gain):
        ss = jnp.sum(t * t, axis=0, keepdims=True)
        tn = t * lax.rsqrt(ss * (1.0 / GROUP) + EPS) * gain
        if rope:
            cos, sin = cos_ref[...], sin_ref[...]
            x1, x2 = tn[:half], tn[half:]
            tn = jnp.concatenate([x1 * cos - x2 * sin, x2 * cos + x1 * sin], axis=0)
        return tn

    for c0 in range(0, d, PROJ_ROWS):
        q_c, k_c = project(c0, PROJ_ROWS), project(d + c0, PROJ_ROWS)
        vt_scr[c0:c0 + PROJ_ROWS, :] = project(2 * d + c0, PROJ_ROWS)
        for gi in range(PROJ_ROWS // GROUP):
            loc = slice(gi * GROUP, (gi + 1) * GROUP)
            rows = slice(c0 + gi * GROUP, c0 + (gi + 1) * GROUP)
            qt_ref[rows, :] = (norm_rope(q_c[loc], qg_ref[rows, :]) * Q_SCALE).astype(BF16)
            kt_scr[rows, :] = norm_rope(k_c[loc], kg_ref[rows, :])
    k_ref[...] = kt_scr[...].T.astype(BF16)
    vt = vt_scr[...]
    ones = jnp.ones((ONES_ROWS, att_t), BF16)
    for c in range(tm // att_t):
        for h in range(d // LANES):
            vt_ref[c, h * V_ROWS:h * V_ROWS + LANES, :] = vt[_head(h), c * att_t:(c + 1) * att_t].astype(BF16)
            vt_ref[c, h * V_ROWS + LANES:(h + 1) * V_ROWS, :] = ones

    def write_f32():
        if n_prev:
            kf_ref[0:n_prev] = pk_ref[...]
            vf_ref[0:n_prev] = pv_ref[...]
        kf_ref[n_prev] = kt_scr[...]
        if vf_layout == "heads":
            v = vt.T
            vf_new = vf_ref.at[n_prev]
            for h in range(d // LANES):
                vf_new[pl.ds(h, tm, stride=d // LANES), :] = v[:, _head(h)]
        else:
            vf_ref[n_prev] = vt

    if keep_t < n_t:
        pl.when(pl.program_id(0) % n_t >= n_t - keep_t)(write_f32)
    else:
        write_f32()


def _project_prompt(x, g, w_in_t, q_gain, k_gain, *, batch, seq, keep, rope_tables=None, vf_layout="cols",
                    prev_k=None, prev_v=None):
    rows, d = x.shape
    n_heads = d // LANES
    tm = _row_tile(math.gcd(seq, keep))
    n_t, keep_t = seq // tm, keep // tm
    att_t = min(ATT_TILE, tm)
    n_prev = 0 if prev_k is None else prev_k.shape[0]
    n = n_prev + 1

    kept_tile = lambda i: jnp.maximum(i % n_t - (n_t - keep_t), 0)
    cols_spec = lambda m: pl.BlockSpec((m, None, d, tm), lambda i: (0, i // n_t, 0, kept_tile(i)))
    heads_spec = lambda m: pl.BlockSpec((m, tm * n_heads, LANES),
                                        lambda i: (0, (i // n_t) * keep_t + kept_tile(i), 0))
    v_f32_spec = heads_spec if vf_layout == "heads" else cols_spec
    v_f32_shape = (n, batch * keep * n_heads, LANES) if vf_layout == "heads" else (n, batch, d, keep)

    row_spec = pl.BlockSpec((tm, d), lambda i: (i, 0))
    in_specs = [row_spec, _const_spec((1, d)), _const_spec((3 * d, d)), _const_spec((d, tm)), _const_spec((d, tm))]
    args = [x, g, w_in_t, q_gain, k_gain]
    if rope_tables is not None:
        cos, sin = rope_tables
        n_rt = cos.shape[1] // tm
        in_specs += [pl.BlockSpec((GROUP // 2, tm), lambda i: (0, i % n_rt))] * 2
        args += [cos, sin]
    if n_prev:
        in_specs += [cols_spec(n_prev), v_f32_spec(n_prev)]
        args += [prev_k, prev_v]
    return pl.pallas_call(
        functools.partial(_proj_prompt_kernel, rope=rope_tables is not None, vf_layout=vf_layout, n_prev=n_prev,
                          n_t=n_t, keep_t=keep_t, att_t=att_t),
        grid=(rows // tm,),
        in_specs=in_specs,
        out_specs=[pl.BlockSpec((None, d, tm), lambda i: (i // n_t, 0, i % n_t)), row_spec,
                   pl.BlockSpec((tm // att_t, n_heads * V_ROWS, att_t), lambda i: (i, 0, 0)), cols_spec(n),
                   v_f32_spec(n)],
        out_shape=[jax.ShapeDtypeStruct((batch, d, seq), BF16), jax.ShapeDtypeStruct((rows, d), BF16),
                   jax.ShapeDtypeStruct((rows // att_t, n_heads * V_ROWS, att_t), BF16),
                   jax.ShapeDtypeStruct((n, batch, d, keep), F32), jax.ShapeDtypeStruct(v_f32_shape, F32)],
        scratch_shapes=[pltpu.VMEM((d, tm), F32), pltpu.VMEM((d, tm), F32)],
        compiler_params=_params("arbitrary"),
        name="qkv_project_prompt",
    )(*args)


def _lambda_value(lp, lam_init):
    return (jnp.exp(jnp.sum(lp[0:1] * lp[1:2], keepdims=True))
            - jnp.exp(jnp.sum(lp[2:3] * lp[3:4], keepdims=True)) + lam_init)


def _reset_state(m_scr, l_scr, acc_scr):
    m_scr[...] = jnp.full(m_scr.shape, NEG, F32)
    l_scr[...] = jnp.zeros(l_scr.shape, F32)
    acc_scr[...] = jnp.zeros(acc_scr.shape, F32)


def _stack_groups_t(qt):
    feat = lax.broadcasted_iota(jnp.int32, qt.shape, 0)
    zero = jnp.zeros_like(qt)
    low = feat < GROUP
    return jnp.concatenate([jnp.where(low, qt, zero), jnp.where(low, zero, qt)], axis=1)


def _reset_state_t(m_scr, acc_scr):
    m_scr[...] = jnp.full(m_scr.shape, NEG, F32)
    acc_scr[...] = jnp.zeros(acc_scr.shape, F32)


def _update_state_t(m_scr, acc_scr, h, s, vt):
    m = m_scr[h]
    m_new = jnp.maximum(m, jnp.max(s, axis=0, keepdims=True))
    p = jnp.exp2(s - m_new)
    acc_scr[h] = jnp.exp2(m - m_new) * acc_scr[h] + jnp.dot(vt, p.astype(BF16), preferred_element_type=F32)
    m_scr[h] = m_new


def _normalised_t(acc):
    return acc[:LANES] * (1.0 / acc[LANES:LANES + 1])


def _state_scratch_t(n_heads, t):
    return [pltpu.VMEM((n_heads, 1, 2 * t), F32), pltpu.VMEM((n_heads, V_ROWS, 2 * t), F32)]


def _stack_groups(qh):
    lane = lax.broadcasted_iota(jnp.int32, qh.shape, 1)
    zero = jnp.zeros_like(qh)
    low = lane < GROUP
    return jnp.concatenate([jnp.where(low, qh, zero), jnp.where(low, zero, qh)], axis=0)


def _dot_nt(a, b):
    return lax.dot_general(a, b, (((1,), (1,)), ((), ())), preferred_element_type=F32)


def _update_state(m_scr, l_scr, acc_scr, h, s, pv):
    m = m_scr[h]
    m_new = jnp.maximum(m, jnp.max(s, axis=1, keepdims=True))
    alpha = jnp.exp2(m - m_new)
    p = jnp.exp2(s - m_new)
    l_scr[h] = alpha * l_scr[h] + jnp.sum(p, axis=1, keepdims=True)
    acc_scr[h] = alpha * acc_scr[h] + pv(p.astype(BF16))
    m_scr[h] = m_new


def _state_scratch(n_heads, t):
    return [pltpu.VMEM((n_heads, 2 * t, 1), F32), pltpu.VMEM((n_heads, 2 * t, 1), F32),
            pltpu.VMEM((n_heads, 2 * t, LANES), F32)]


def _diff_combine(on, t, lam, sub_gain, lam_init):
    o = on[:t] - lam * on[t:]
    return _rms(o, sub_gain) * (1.0 - lam_init)


def _select_groups(on, t):
    lane = lax.broadcasted_iota(jnp.int32, (t, LANES), 1)
    return jnp.where(lane < GROUP, on[:t], on[t:])


def _diff_prompt_kernel(qt_ref, k_ref, vt_ref, x_ref, wo_ref, sg_ref, lp_ref, o_ref,
                        qs_scr, m_scr, acc_scr, s_scr, mask_scr, obuf, *, lam_init):
    i = pl.program_id(1)
    d, t = qt_ref.shape
    n_heads = d // LANES

    @pl.when((pl.program_id(0) == 0) & (i == 0))
    def _():
        key = lax.broadcasted_iota(jnp.int32, (t, 2 * t), 0)
        qry = lax.broadcasted_iota(jnp.int32, (t, 2 * t), 1) % t
        mask_scr[...] = jnp.where((key // CHUNK) <= (qry // CHUNK), 0.0, NEG)

    for h in range(n_heads):
        qs_scr[h] = _stack_groups_t(qt_ref[_head(h), :])
    _reset_state_t(m_scr, acc_scr)

    def scores(j, slot, diagonal=False):
        rows = pl.ds(pl.multiple_of(j * t, t), t)
        for h in range(n_heads):
            s = jnp.dot(k_ref[rows, _head(h)], qs_scr[h], preferred_element_type=F32)
            s_scr[slot, h] = s + mask_scr[...] if diagonal else s

    def consume(j, slot):
        for h in range(n_heads):
            _update_state_t(m_scr, acc_scr, h, s_scr[slot, h], vt_ref[j, h * V_ROWS:(h + 1) * V_ROWS, :])

    pairs = jnp.maximum(i - 1, 0) // 2
    pl.when(i > 0)(lambda: scores(0, 0))

    def body(p, c):
        j = 2 * p
        scores(j + 1, 1)
        consume(j, 0)
        scores(j + 2, 0)
        consume(j + 1, 1)
        return c

    lax.fori_loop(0, pairs, body, 0)
    left = i - 2 * pairs

    @pl.when(left == 0)
    def _():
        scores(i, 0, True)
        consume(i, 0)

    @pl.when(left == 1)
    def _():
        scores(i, 1, True)
        consume(i - 1, 0)
        consume(i, 1)

    @pl.when(left == 2)
    def _():
        scores(i - 1, 1)
        consume(i - 2, 0)
        scores(i, 0, True)
        consume(i - 1, 1)
        consume(i, 0)

    lam = _lambda_value(lp_ref[...], lam_init)
    for h in range(n_heads):
        on = _normalised_t(acc_scr[h])
        o = on[:, :t] - lam * on[:, t:]
        o = o * lax.rsqrt(jnp.mean(o * o, axis=0, keepdims=True) + EPS)
        obuf[:, _head(h)] = (o.T * sg_ref[:, _head(h)] * (1.0 - lam_init)).astype(BF16)
    o_ref[...] = x_ref[...] + jnp.dot(obuf[...], wo_ref[...], preferred_element_type=F32)


def _diff_attention_prompt(qt, k, vt, x, w_out, sub_gain, lam_p, *, batch, seq, lam_init):
    rows, d = x.shape
    t = vt.shape[2]
    nq = seq // t
    n_heads = d // LANES
    row_spec = pl.BlockSpec((t, d), lambda b, i: (b * nq + i, 0))
    return pl.pallas_call(
        functools.partial(_diff_prompt_kernel, lam_init=lam_init),
        grid=(batch, nq),
        in_specs=[pl.BlockSpec((None, d, t), lambda b, i: (b, 0, i)),
                  pl.BlockSpec((seq, d), lambda b, i: (b, 0)),
                  pl.BlockSpec((nq, n_heads * V_ROWS, t), lambda b, i: (b, 0, 0)),
                  row_spec, _const_spec((d, d)), _const_spec((1, d)), _const_spec(lam_p.shape)],
        out_specs=row_spec,
        out_shape=jax.ShapeDtypeStruct((rows, d), F32),
        scratch_shapes=[pltpu.VMEM((n_heads, LANES, 2 * t), BF16)] + _state_scratch_t(n_heads, t)
        + [pltpu.VMEM((2, n_heads, t, 2 * t), F32), pltpu.VMEM((t, 2 * t), F32), pltpu.VMEM((t, d), BF16)],
        compiler_params=_params("arbitrary", "arbitrary"),
        name="diff_attention_prompt",
    )(qt, k, vt, x, w_out, sub_gain, lam_p)


def _diff_decode_kernel(q_ref, kc_ref, vc_ref, kn_ref, vn_ref, x_ref, wo_ref, sg_ref, lp_ref, o_ref,
                        m_scr, l_scr, acc_scr, obuf, *, lam_init):
    j = pl.program_id(1)
    t, d = q_ref.shape
    n_heads = d // LANES
    tk = kc_ref.shape[1]

    @pl.when(j == 0)
    def _():
        _reset_state(m_scr, l_scr, acc_scr)

    for h in range(n_heads):
        qs = _stack_groups(q_ref[:, _head(h)])
        s = jnp.dot(qs, kc_ref[_head(h), :].astype(BF16), preferred_element_type=F32)
        vh = vc_ref[pl.ds(h, tk, stride=n_heads), :].astype(BF16)
        _update_state(m_scr, l_scr, acc_scr, h, s, lambda p: jnp.dot(p, vh, preferred_element_type=F32))

    @pl.when(j == pl.num_programs(1) - 1)
    def _():
        lam = _lambda_value(lp_ref[...], lam_init)
        for h in range(n_heads):
            qs = _stack_groups(q_ref[:, _head(h)])
            vh = vn_ref[:, _head(h)]
            _update_state(m_scr, l_scr, acc_scr, h, _dot_nt(qs, kn_ref[:, _head(h)]),
                          lambda p: jnp.dot(p, vh, preferred_element_type=F32))
            on = acc_scr[h] / l_scr[h]
            obuf[:, _head(h)] = _diff_combine(on, t, lam, sg_ref[:, _head(h)], lam_init).astype(BF16)
        o_ref[...] = x_ref[...] + jnp.dot(obuf[...], wo_ref[...], preferred_element_type=F32)


def _diff_attention_decode(q, cache_kt, cache_vh, k_new, v_new, x, w_out, sub_gain, lam_p, *, layer, lam_init):
    _, batch, d, past = cache_kt.shape
    rows = x.shape[0]
    t = rows // batch
    tk = min(DEC_KEY_TILE, past)
    n_heads = d // LANES
    row_spec = pl.BlockSpec((t, d), lambda b, j: (b, 0))
    kt_spec = pl.BlockSpec((None, None, d, tk), lambda b, j: (layer, b, 0, j))
    v_spec = pl.BlockSpec((None, None, tk * n_heads, LANES), lambda b, j: (layer, b, j, 0))
    return pl.pallas_call(
        functools.partial(_diff_decode_kernel, lam_init=lam_init),
        grid=(batch, past // tk),
        in_specs=[row_spec, kt_spec, v_spec, row_spec, row_spec, row_spec, _const_spec((d, d)),
                  _const_spec((1, d)), _const_spec(lam_p.shape)],
        out_specs=row_spec,
        out_shape=jax.ShapeDtypeStruct((rows, d), F32),
        scratch_shapes=_state_scratch(n_heads, t) + [pltpu.VMEM((t, d), BF16)],
        compiler_params=_params("arbitrary", "arbitrary"),
        name="diff_attention_decode",
    )(q, cache_kt, cache_vh, k_new, v_new, x, w_out, sub_gain, lam_p)


def _gather_rows(tbl, idx_of_col, width):
    n_pad = tbl.shape[1]
    t_idx = lax.broadcasted_iota(jnp.int32, (n_pad, width), 0)
    v_idx = lax.broadcasted_iota(jnp.int32, (n_pad, width), 1)
    onehot = jnp.where(t_idx == idx_of_col(v_idx), 1.0, 0.0).astype(BF16)
    hi = tbl.astype(BF16)
    r1 = tbl - hi.astype(F32)
    mid = r1.astype(BF16)
    lo = (r1 - mid.astype(F32)).astype(BF16)
    dot = lambda a: jnp.dot(a, onehot, preferred_element_type=F32)
    return dot(hi) + dot(mid) + dot(lo)


def _rel_index(rel):
    return jnp.clip(rel, -C_REL_CLIP, C_REL_CLIP) + C_REL_CLIP


def _toeplitz(vec, rows, cols, origin):
    width = vec.shape[1]
    xb = jnp.broadcast_to(vec, (rows, width))
    return pltpu.roll(xb, (width - origin) % width, 1, stride=1, stride_axis=0)[:, :cols]


def _bias_kernel(tbl_ref, bp_ref, bs_ref, *, t, n_blocks, ts, win):
    tbl = tbl_ref[...] * LOG2E
    heads = tbl.shape[0]
    key = lax.broadcasted_iota(jnp.int32, (t, t), 0)
    qry = lax.broadcasted_iota(jnp.int32, (t, t), 1)
    for jb in range(n_blocks):
        off = (n_blocks - 1 - jb) * t
        g = _gather_rows(tbl, lambda v: _rel_index(v - (t - 1) + off), 2 * t)
        key_chunk = (jb * t + key) // CHUNK - (n_blocks - 1) * t // CHUNK
        q_chunk = qry // CHUNK
        visible = (key_chunk <= q_chunk) & (key_chunk >= q_chunk - C_LEFT_CHUNKS)
        for h in range(heads):
            tile = _toeplitz(g[h:h + 1, :], t, t, t - 1)
            bp_ref[jb, h // 2, :, (h % 2) * t:(h % 2 + 1) * t] = jnp.where(visible, tile, NEG)
    width = bs_ref.shape[-1]
    wide = 1 << (width + ts).bit_length()
    g = _gather_rows(tbl, lambda v: _rel_index(win + ts - 1 - v), wide)
    for h in range(heads):
        tile = _toeplitz(g[h:h + 1, :], ts, width, ts - 1)
        bs_ref[h // 2, (h % 2) * ts:(h % 2 + 1) * ts, :] = tile


def _band_bias(table_t, *, t, n_blocks, ts, win):
    heads, n = table_t.shape
    n_pad = -(-n // LANES) * LANES
    tbl = jnp.pad(table_t, ((0, 0), (0, n_pad - n)))
    return pl.pallas_call(
        functools.partial(_bias_kernel, t=t, n_blocks=n_blocks, ts=ts, win=win),
        out_shape=[jax.ShapeDtypeStruct((n_blocks, heads // 2, t, 2 * t), F32),
                   jax.ShapeDtypeStruct((heads // 2, 2 * ts, win + ts), F32)],
        compiler_params=pltpu.CompilerParams(vmem_limit_bytes=VMEM_LIMIT_BYTES),
        name="band_bias",
    )(tbl)


def _band_prompt_kernel(qt_ref, k_ref, vt_ref, b_ref, x_ref, wo_ref, o_ref, qs_scr, m_scr, acc_scr, s_scr, obuf):
    i = pl.program_id(1)
    d, t = qt_ref.shape
    n_heads = d // LANES
    n_blocks = b_ref.shape[0]
    first = jnp.maximum(n_blocks - 1 - i, 0)
    for h in range(n_heads):
        qs_scr[h] = _stack_groups_t(qt_ref[_head(h), :])
    _reset_state_t(m_scr, acc_scr)

    def scores(jb):
        rows = pl.ds(pl.multiple_of((i - (n_blocks - 1) + jb) * t, t), t)
        for h in range(n_heads):
            s_scr[jb % 2, h] = (jnp.dot(k_ref[rows, _head(h)], qs_scr[h], preferred_element_type=F32)
                                + b_ref[jb, h])

    def consume(jb):
        j = i - (n_blocks - 1) + jb
        for h in range(n_heads):
            _update_state_t(m_scr, acc_scr, h, s_scr[jb % 2, h], vt_ref[j, h * V_ROWS:(h + 1) * V_ROWS, :])

    def window(f):
        scores(f)
        for jb in range(f, n_blocks):
            if jb + 1 < n_blocks:
                scores(jb + 1)
            consume(jb)

    for f in range(n_blocks):
        pl.when(first == f)(functools.partial(window, f))
    feat = lax.broadcasted_iota(jnp.int32, (LANES, t), 0)
    for h in range(n_heads):
        on = _normalised_t(acc_scr[h])
        obuf[:, _head(h)] = jnp.where(feat < GROUP, on[:, :t], on[:, t:]).T.astype(BF16)
    o_ref[...] = x_ref[...] + jnp.dot(obuf[...], wo_ref[...], preferred_element_type=F32)


def _band_attention_prompt(qt, k, vt, bias, x, w_out, *, batch, seq):
    rows, d = x.shape
    t = vt.shape[2]
    nq = seq // t
    n_heads = d // LANES
    row_spec = pl.BlockSpec((t, d), lambda b, i: (b * nq + i, 0))
    return pl.pallas_call(
        _band_prompt_kernel,
        grid=(batch, nq),
        in_specs=[pl.BlockSpec((None, d, t), lambda b, i: (b, 0, i)),
                  pl.BlockSpec((seq, d), lambda b, i: (b, 0)),
                  pl.BlockSpec((nq, n_heads * V_ROWS, t), lambda b, i: (b, 0, 0)),
                  _const_spec(bias.shape), row_spec, _const_spec((d, d))],
        out_specs=row_spec,
        out_shape=jax.ShapeDtypeStruct((rows, d), F32),
        scratch_shapes=[pltpu.VMEM((n_heads, LANES, 2 * t), BF16)] + _state_scratch_t(n_heads, t)
        + [pltpu.VMEM((2, n_heads, t, 2 * t), F32), pltpu.VMEM((t, d), BF16)],
        compiler_params=_params("arbitrary", "arbitrary"),
        name="band_attention_prompt",
    )(qt, k, vt, bias, x, w_out)


def _band_decode_kernel(q_ref, kc_ref, vc_ref, kn_ref, vn_ref, b_ref, x_ref, wo_ref, o_ref, obuf):
    t, d = q_ref.shape
    win = kc_ref.shape[1]
    for h in range(d // LANES):
        qs = _stack_groups(q_ref[:, _head(h)])
        bias = b_ref[h]
        s_c = jnp.dot(qs, kc_ref[_head(h), :].astype(BF16), preferred_element_type=F32) + bias[:, :win]
        s_n = _dot_nt(qs, kn_ref[:, _head(h)]) + bias[:, win:]
        m = jnp.maximum(jnp.max(s_c, axis=1, keepdims=True), jnp.max(s_n, axis=1, keepdims=True))
        p_c = jnp.exp2(s_c - m)
        p_n = jnp.exp2(s_n - m)
        l = jnp.sum(p_c, axis=1, keepdims=True) + jnp.sum(p_n, axis=1, keepdims=True)
        acc = (_dot_nt(p_c.astype(BF16), vc_ref[_head(h), :].astype(BF16))
               + jnp.dot(p_n.astype(BF16), vn_ref[:, _head(h)], preferred_element_type=F32))
        obuf[:, _head(h)] = _select_groups(acc / l, t).astype(BF16)
    o_ref[...] = x_ref[...] + jnp.dot(obuf[...], wo_ref[...], preferred_element_type=F32)


def _band_attention_decode(q, cache_kt, cache_vt, k_new, v_new, bias, x, w_out, *, layer):
    _, batch, d, win = cache_kt.shape
    rows = x.shape[0]
    t = rows // batch
    row_spec = pl.BlockSpec((t, d), lambda b: (b, 0))
    cache_spec = pl.BlockSpec((None, None, d, win), lambda b: (layer, b, 0, 0))
    return pl.pallas_call(
        _band_decode_kernel,
        grid=(batch,),
        in_specs=[row_spec, cache_spec, cache_spec, row_spec, row_spec, _const_spec(bias.shape), row_spec,
                  _const_spec((d, d))],
        out_specs=row_spec,
        out_shape=jax.ShapeDtypeStruct((rows, d), F32),
        scratch_shapes=[pltpu.VMEM((t, d), BF16)],
        compiler_params=_params("arbitrary"),
        name="band_attention_decode",
    )(q, cache_kt, cache_vt, k_new, v_new, bias, x, w_out)


def _gelu_tanh(x):
    c = -2.0 * math.sqrt(2.0 / math.pi) * LOG2E
    return x / (1.0 + jnp.exp2(x * (c + (0.044715 * c) * (x * x))))


def _rglru_kernel(x_ref, g_ref, win_ref, bin_ref, cw_ref, cb_ref, gaw_ref, gab_ref, gxw_ref, gxb_ref,
                  lam_ref, wout_ref, hist_ref, h0_ref, o_ref, cst_ref, hst_ref, ubuf, hcar, h_scr):
    tm, w = h_scr.shape
    pad = SUBLANES

    @pl.when(pl.program_id(1) == 0)
    def _():
        ubuf[...] = hist_ref[...]
        hcar[...] = h0_ref[...]

    x = x_ref[...]
    xn = _rms(x, g_ref[...]).astype(BF16)
    gu = jnp.dot(xn, win_ref[...], preferred_element_type=F32) + bin_ref[...]
    gate = _gelu_tanh(gu[:, :w])
    u = gu[:, w:]
    cw = cw_ref[...]
    prev = ubuf[...]
    sub8 = lax.broadcasted_iota(jnp.int32, (pad, w), 0)
    xc = cb_ref[...] + cw[3:4] * u
    for j in range(B_CONV_W - 1):
        back = B_CONV_W - 1 - j
        rolled = pltpu.roll(u, back, 0)
        head = jnp.where(sub8 < back, pltpu.roll(prev, back, 0), rolled[0:pad])
        xc = xc + cw[j:j + 1] * jnp.concatenate([head, rolled[pad:]], axis=0)

    xcb = xc.astype(BF16)
    bw = w // B_BLOCKS
    r_parts, i_parts = [], []
    for n in range(B_BLOCKS):
        blk = xcb[:, n * bw:(n + 1) * bw]
        r_parts.append(jnp.dot(blk, gaw_ref[n], preferred_element_type=F32))
        i_parts.append(jnp.dot(blk, gxw_ref[n], preferred_element_type=F32))
    r = jax.nn.sigmoid(jnp.concatenate(r_parts, axis=1) + gab_ref[...])
    ig = jax.nn.sigmoid(jnp.concatenate(i_parts, axis=1) + gxb_ref[...])
    z = -lam_ref[...]
    softplus = jnp.maximum(z, 0.0) + jnp.log1p(jnp.exp(-jnp.abs(z)))
    a = jnp.exp(-B_C * r * softplus)
    b = jnp.sqrt(1.0 - a * a) * (ig * xc)

    sub = lax.broadcasted_iota(jnp.int32, (tm, w), 0) % SUBLANES
    def group_roll(v, s):
        return pltpu.roll(v.reshape(tm // SUBLANES, SUBLANES, w), s, 1).reshape(tm, w)

    for s in (1, 2, 4):
        a_prev = group_roll(a, s)
        b_prev = group_roll(b, s)
        inside = sub >= s
        b = jnp.where(inside, a * b_prev + b, b)
        a = jnp.where(inside, a * a_prev, a)
    h_prev = hcar[...]
    for gidx in range(tm // SUBLANES):
        rows = slice(gidx * SUBLANES, (gidx + 1) * SUBLANES)
        hg = a[rows] * h_prev + b[rows]
        h_scr[rows, :] = hg
        h_prev = jnp.broadcast_to(hg[SUBLANES - 1:SUBLANES, :], (SUBLANES, w))
    hcar[...] = h_prev
    hs = h_scr[...]

    y = jnp.dot((hs * gate).astype(BF16), wout_ref[...], preferred_element_type=F32)
    o_ref[...] = x + y
    ubuf[...] = u[tm - pad:tm, :]
    cst_ref[...] = u[tm - pad:tm, :]
    hst_ref[...] = h_prev


def _rglru(x, g, p, hist, h0, *, batch, seq):
    rows, d = x.shape
    w = p["w_out"].shape[0]
    tm = _row_tile(seq)
    nt = seq // tm
    row_spec = pl.BlockSpec((tm, d), lambda b, t: (b * nt + t, 0))
    state_spec = pl.BlockSpec((None, SUBLANES, w), lambda b, t: (b, 0, 0))
    consts = [g, p["w_in"], p["b_in"], p["conv_w"], p["conv_b"], p["ga_w"], p["ga_b"], p["gx_w"], p["gx_b"],
              p["lam"], p["w_out"]]
    return pl.pallas_call(
        _rglru_kernel,
        grid=(batch, nt),
        in_specs=[row_spec] + [_const_spec(c.shape) for c in consts] + [state_spec, state_spec],
        out_specs=[row_spec, state_spec, state_spec],
        out_shape=[jax.ShapeDtypeStruct((rows, d), F32), jax.ShapeDtypeStruct((batch, SUBLANES, w), F32),
                   jax.ShapeDtypeStruct((batch, SUBLANES, w), F32)],
        scratch_shapes=[pltpu.VMEM((SUBLANES, w), F32), pltpu.VMEM((SUBLANES, w), F32), pltpu.VMEM((tm, w), F32)],
        compiler_params=_params("arbitrary", "arbitrary"),
        name="rglru_block",
    )(x, *consts, hist, h0)


def _mlp_kernel(x_ref, g_ref, w1_ref, w2_ref, gf_ref, o_ref, *, final_norm):
    x = x_ref[...]
    xn = _rms(x, g_ref[...]).astype(BF16)
    acc = x
    for c in range(w1_ref.shape[1] // FF_CHUNK):
        cols = slice(c * FF_CHUNK, (c + 1) * FF_CHUNK)
        h = jnp.dot(xn, w1_ref[:, cols], preferred_element_type=F32)
        h = jnp.square(jnp.maximum(h, 0.0)).astype(BF16)
        acc = acc + jnp.dot(h, w2_ref[cols, :], preferred_element_type=F32)
    if final_norm:
        acc = _rms(acc, gf_ref[...])
    o_ref[...] = acc


def _mlp(x, g, w1, w2, g_final, *, final_norm):
    rows, d = x.shape
    tm = _row_tile(rows, MLP_TILE)
    row_spec = pl.BlockSpec((tm, d), lambda i: (i, 0))
    return pl.pallas_call(
        functools.partial(_mlp_kernel, final_norm=final_norm),
        grid=(rows // tm,),
        in_specs=[row_spec, _const_spec((1, d)), _const_spec(w1.shape), _const_spec(w2.shape),
                  _const_spec((1, d))],
        out_specs=row_spec,
        out_shape=jax.ShapeDtypeStruct((rows, d), F32),
        compiler_params=_params("arbitrary"),
        name="mlp",
    )(x, g, w1, w2, g_final)


def _rope_angles(pos):
    half = GROUP // 2
    inv = ROPE_THETA ** (-jnp.arange(half, dtype=F32) / half)
    return pos.astype(F32)[:, None] * inv[None, :]


def _rope_tables_rows(pos):
    ang = _rope_angles(pos)
    cos = jnp.tile(jnp.cos(ang), (1, LANES // ang.shape[1]))
    sin = jnp.tile(jnp.concatenate([-jnp.sin(ang), jnp.sin(ang)], axis=1), (1, LANES // GROUP))
    return cos, sin


def _rope_tables_cols(pos):
    ang = _rope_angles(pos).T
    return jnp.cos(ang), jnp.sin(ang)


def _feature_major(cache):
    n, batch, pos = cache.shape[:3]
    nd = cache.ndim
    return jnp.transpose(cache, (0, 1) + tuple(range(3, nd)) + (2,)).reshape(n, batch, -1, pos)


def _position_major(x, feature_dims):
    n, batch, _, pos = x.shape
    nf = len(feature_dims)
    return jnp.transpose(x.reshape(n, batch, *feature_dims, pos), (0, 1, 2 + nf) + tuple(range(2, 2 + nf)))


def kernel(x_prompt, x_sample, cache_a_k, cache_a_v, state_b_conv, state_b_h, cache_c_k, cache_c_v, norm_mix_g, norm_mlp_g, norm_final_g, a_w_in, a_q_norm_g, a_k_norm_g, a_lambda, a_subln_g, a_w_out, b_w_in, b_b_in, b_conv_w, b_conv_b, b_gate_a_w, b_gate_a_b, b_gate_x_w, b_gate_x_b, b_lambda, b_w_out, c_w_in, c_q_norm_g, c_k_norm_g, c_rel_bias, c_w_out, mlp_w1, mlp_w2):
    batch, seq, d = x_prompt.shape
    dbatch, dseq, _ = x_sample.shape
    depth = norm_mix_g.shape[0]
    past = cache_a_k.shape[2]
    n_heads = d // LANES
    xp = x_prompt.reshape(batch * seq, d)
    xs = x_sample.reshape(dbatch * dseq, d)
    row = lambda v: v.reshape(1, -1).astype(F32)
    tile_gain = lambda v: jnp.tile(v.astype(F32), d // v.shape[0]).reshape(1, d)
    grp = jnp.arange(LANES) // GROUP
    bd = (grp[:, None] == grp[None, :]).astype(BF16)
    col_gain = lambda v: jnp.broadcast_to(tile_gain(v).reshape(d, 1), (d, min(ROW_TILE, seq)))
    rope_p = _rope_tables_cols(jnp.arange(seq))
    rope_s = tuple(jnp.tile(t, (dbatch, 1)) for t in _rope_tables_rows(past + jnp.arange(dseq)))
    keep = min(C_BAND_PAST, seq)
    att_t = min(ATT_TILE, seq)
    n_band_blocks = C_BAND_PAST // att_t + 1
    a_kt = _feature_major(cache_a_k)
    a_vh = cache_a_v.reshape(cache_a_v.shape[0], dbatch, past * n_heads, LANES)
    c_kt, c_vt = _feature_major(cache_c_k), _feature_major(cache_c_v)
    c_heads = c_rel_bias.shape[2]

    outs = {name: [] for name in ("aks", "avs", "bcp", "bhp", "bcs", "bhs", "cks", "cvs")}
    akp = avp = ckp = cvp = None
    for layer in range(depth):
        kind, idx = layer % 3, layer // 3
        g_mix = row(norm_mix_g[layer])
        if kind == 0:
            lam_init = 0.8 - 0.6 * math.exp(-0.3 * layer)
            w_in = a_w_in[idx].astype(BF16)
            w_out = a_w_out[idx].astype(BF16)
            qg, kg, sg = tile_gain(a_q_norm_g[idx]), tile_gain(a_k_norm_g[idx]), tile_gain(a_subln_g[idx])
            lam_p = a_lambda[idx].astype(F32)
            qt, k, vt, akp, avp = _project_prompt(xp, g_mix, w_in.T, col_gain(a_q_norm_g[idx]),
                                                  col_gain(a_k_norm_g[idx]), batch=batch, seq=seq, keep=seq,
                                                  rope_tables=rope_p, vf_layout="heads", prev_k=akp, prev_v=avp)
            xp = _diff_attention_prompt(qt, k, vt, xp, w_out, sg, lam_p, batch=batch, seq=seq, lam_init=lam_init)
            q, k, v, kf, vf = _project_rows(xs, g_mix, w_in, qg, kg, bd, rope_tables=rope_s)
            xs = _diff_attention_decode(q, a_kt, a_vh, k, v, xs, w_out, sg, lam_p, layer=idx, lam_init=lam_init)
            outs["aks"].append(kf.reshape(dbatch, dseq, n_heads, 2, GROUP))
            outs["avs"].append(vf.reshape(dbatch, dseq, n_heads, LANES))
        elif kind == 1:
            w = b_w_out.shape[1]
            p = dict(w_in=b_w_in[idx].astype(BF16), b_in=row(b_b_in[idx]), conv_w=b_conv_w[idx].astype(F32),
                     conv_b=row(b_conv_b[idx]), ga_w=b_gate_a_w[idx].astype(BF16), ga_b=row(b_gate_a_b[idx]),
                     gx_w=b_gate_x_w[idx].astype(BF16), gx_b=row(b_gate_x_b[idx]), lam=row(b_lambda[idx]),
                     w_out=b_w_out[idx].astype(BF16))
            zeros = jnp.zeros((batch, SUBLANES, w), F32)
            xp, cst, hst = _rglru(xp, g_mix, p, zeros, zeros, batch=batch, seq=seq)
            outs["bcp"].append(cst[:, SUBLANES - (B_CONV_W - 1):])
            outs["bhp"].append(hst[:, 0])
            hist = jnp.pad(state_b_conv[idx].astype(F32), ((0, 0), (SUBLANES - (B_CONV_W - 1), 0), (0, 0)))
            h0 = jnp.broadcast_to(state_b_h[idx].astype(F32)[:, None, :], (dbatch, SUBLANES, w))
            xs, cst, hst = _rglru(xs, g_mix, p, hist, h0, batch=dbatch, seq=dseq)
            outs["bcs"].append(cst[:, SUBLANES - (B_CONV_W - 1):])
            outs["bhs"].append(hst[:, 0])
        else:
            w_in = c_w_in[idx].astype(BF16)
            w_out = c_w_out[idx].astype(BF16)
            qg, kg = tile_gain(c_q_norm_g[idx]), tile_gain(c_k_norm_g[idx])
            win = c_kt.shape[3]
            bias_p, bias_s = _band_bias(c_rel_bias[idx].astype(F32).T, t=att_t, n_blocks=n_band_blocks, ts=dseq,
                                        win=win)
            qt, k, vt, ckp, cvp = _project_prompt(xp, g_mix, w_in.T, col_gain(c_q_norm_g[idx]),
                                                  col_gain(c_k_norm_g[idx]), batch=batch, seq=seq, keep=keep,
                                                  prev_k=ckp, prev_v=cvp)
            xp = _band_attention_prompt(qt, k, vt, bias_p, xp, w_out, batch=batch, seq=seq)
            q, k, v, kf, vf = _project_rows(xs, g_mix, w_in, qg, kg, bd)
            xs = _band_attention_decode(q, c_kt, c_vt, k, v, bias_s, xs, w_out, layer=idx)
            outs["cks"].append(kf.reshape(dbatch, dseq, c_heads, GROUP))
            outs["cvs"].append(vf.reshape(dbatch, dseq, c_heads, GROUP))
        g_mlp = row(norm_mlp_g[layer])
        w1 = mlp_w1[layer].astype(BF16)
        w2 = mlp_w2[layer].astype(BF16)
        last = layer == depth - 1
        xp = _mlp(xp, g_mlp, w1, w2, row(norm_final_g), final_norm=last)
        xs = _mlp(xs, g_mlp, w1, w2, row(norm_final_g), final_norm=last)
    st = lambda name: jnp.stack(outs[name])
    return (xp.reshape(batch, seq, d), xs.reshape(dbatch, dseq, d),
            _position_major(akp, (n_heads, 2, GROUP)), avp.reshape(-1, batch, seq, n_heads, LANES),
            st("aks"), st("avs"),
            st("bcp"), st("bhp"), st("bcs"), st("bhs"),
            _position_major(ckp, (c_heads, GROUP)), _position_major(cvp, (c_heads, GROUP)),
            st("cks"), st("cvs"))
```

```python
import functools
import math

import jax
import jax.numpy as jnp
from jax import lax
from jax.experimental import pallas as pl
from jax.experimental.pallas import tpu as pltpu

F32 = jnp.float32
BF16 = jnp.bfloat16

EPS = 1e-6
CHUNK = 64
GROUP = 64
LANES = 128
SUBLANES = 8
ROPE_THETA = 10000.0
LOG2E = 1.0 / math.log(2.0)
Q_SCALE = GROUP ** -0.5 * LOG2E
B_C = 8.0
B_CONV_W = 4
B_BLOCKS = 4
C_LEFT_CHUNKS = 8
C_BAND_PAST = C_LEFT_CHUNKS * CHUNK
C_REL_CLIP = 128
NEG = -1e30
VMEM_LIMIT_BYTES = 56 * 1024 * 1024

ROW_TILE = 512
MLP_TILE = 1024
ATT_TILE = 256
DEC_KEY_TILE = 2048
FF_CHUNK = 512
PROJ_ROWS = 256
ONES_ROWS = 16
V_ROWS = LANES + ONES_ROWS


def _row_tile(rows, target=ROW_TILE):
    t = min(rows, target)
    assert rows % t == 0, (rows, t)
    return t


def _params(*semantics):
    return pltpu.CompilerParams(dimension_semantics=semantics, vmem_limit_bytes=VMEM_LIMIT_BYTES)


def _const_spec(shape):
    nd = len(shape)
    return pl.BlockSpec(shape, lambda *_: (0,) * nd, pipeline_mode=pl.Buffered(1))


def _rms(x, g):
    ms = jnp.mean(x * x, axis=-1, keepdims=True)
    return x * lax.rsqrt(ms + EPS) * g


def _head(h):
    return slice(h * LANES, (h + 1) * LANES)


def _proj_rows_kernel(*refs, rope):
    it = iter(refs)
    x_ref, g_ref, w_ref, qg_ref, kg_ref, bd_ref = (next(it) for _ in range(6))
    cos_ref, sin_ref = (next(it), next(it)) if rope else (None, None)
    q_ref, k_ref, v_ref, kf_ref, vf_ref = (next(it) for _ in range(5))
    x = x_ref[...]
    tm, d = x.shape
    xn = _rms(x, g_ref[...]).astype(BF16)
    qkv = jnp.dot(xn, w_ref[...], preferred_element_type=F32)
    bd = bd_ref[...]

    def group_norm(t, gain):
        sq = (t * t).astype(BF16)
        ss = jnp.concatenate([jnp.dot(sq[:, _head(j)], bd, preferred_element_type=F32)
                              for j in range(d // LANES)], axis=1)
        return t * lax.rsqrt(ss * (1.0 / GROUP) + EPS) * gain

    def rotary(t):
        cos = cos_ref[...]
        sin = sin_ref[...]
        lane = lax.broadcasted_iota(jnp.int32, (tm, LANES), 1)
        first_half = (lane % GROUP) < (GROUP // 2)
        outs = []
        for j in range(d // LANES):
            tj = t[:, _head(j)]
            partner = jnp.where(first_half, pltpu.roll(tj, LANES - GROUP // 2, 1),
                                pltpu.roll(tj, GROUP // 2, 1))
            outs.append(tj * cos + partner * sin)
        return jnp.concatenate(outs, axis=1)

    q = group_norm(qkv[:, :d], qg_ref[...])
    k = group_norm(qkv[:, d:2 * d], kg_ref[...])
    v = qkv[:, 2 * d:]
    if rope:
        q = rotary(q)
        k = rotary(k)
    q_ref[...] = (q * Q_SCALE).astype(BF16)
    k_ref[...] = k.astype(BF16)
    v_ref[...] = v.astype(BF16)
    kf_ref[...] = k
    vf_ref[...] = v


def _project_rows(x, g, w_in, q_gain, k_gain, bd, rope_tables=None):
    rows, d = x.shape
    tm = _row_tile(rows)
    row_spec = pl.BlockSpec((tm, d), lambda i: (i, 0))
    in_specs = [row_spec, _const_spec((1, d)), _const_spec((d, 3 * d)), _const_spec((1, d)),
                _const_spec((1, d)), _const_spec((LANES, LANES))]
    args = [x, g, w_in, q_gain, k_gain, bd]
    if rope_tables is not None:
        in_specs += [pl.BlockSpec((tm, LANES), lambda i: (i, 0))] * 2
        args += list(rope_tables)
    return pl.pallas_call(
        functools.partial(_proj_rows_kernel, rope=rope_tables is not None),
        grid=(rows // tm,),
        in_specs=in_specs,
        out_specs=[row_spec] * 5,
        out_shape=[jax.ShapeDtypeStruct((rows, d), BF16)] * 3 + [jax.ShapeDtypeStruct((rows, d), F32)] * 2,
        compiler_params=_params("arbitrary"),
        name="qkv_project_rows",
    )(*args)


def _proj_prompt_kernel(*refs, rope, vf_layout, n_prev, n_t, keep_t, att_t):
    it = iter(refs)
    x_ref, g_ref, wt_ref, qg_ref, kg_ref = (next(it) for _ in range(5))
    cos_ref, sin_ref = (next(it), next(it)) if rope else (None, None)
    pk_ref, pv_ref = (next(it), next(it)) if n_prev else (None, None)
    qt_ref, k_ref, vt_ref, kf_ref, vf_ref, kt_scr, vt_scr = (next(it) for _ in range(7))
    x = x_ref[...]
    tm, d = x.shape
    xn = _rms(x, g_ref[...]).astype(BF16)
    half = GROUP // 2

    def project(first_row, n_rows):
        return lax.dot_general(wt_ref[first_row:first_row + n_rows, :], xn, (((1,), (1,)), ((), ())),
                               preferred_element_type=F32)

    def norm_rope(t, gain):
        ss = jnp.sum(t * t, axis=0, keepdims=True)
        tn = t * lax.rsqrt(ss * (1.0 / GROUP) + EPS) * gain
        if rope:
            cos, sin = cos_ref[...], sin_ref[...]
            x1, x2 = tn[:half], tn[half:]
            tn = jnp.concatenate([x1 * cos - x2 * sin, x2 * cos + x1 * sin], axis=0)
        return tn

    for c0 in range(0, d, PROJ_ROWS):
        q_c, k_c = project(c0, PROJ_ROWS), project(d + c0, PROJ_ROWS)
        vt_scr[c0:c0 + PROJ_ROWS, :] = project(2 * d + c0, PROJ_ROWS)
        for gi in range(PROJ_ROWS // GROUP):
            loc = slice(gi * GROUP, (gi + 1) * GROUP)
            rows = slice(c0 + gi * GROUP, c0 + (gi + 1) * GROUP)
            qt_ref[rows, :] = (norm_rope(q_c[loc], qg_ref[rows, :]) * Q_SCALE).astype(BF16)
            kt_scr[rows, :] = norm_rope(k_c[loc], kg_ref[rows, :])
    k_ref[...] = kt_scr[...].T.astype(BF16)
    vt = vt_scr[...]
    ones = jnp.ones((ONES_ROWS, att_t), BF16)
    for c in range(tm // att_t):
        for h in range(d // LANES):
            vt_ref[c, h * V_ROWS:h * V_ROWS + LANES, :] = vt[_head(h), c * att_t:(c + 1) * att_t].astype(BF16)
            vt_ref[c, h * V_ROWS + LANES:(h + 1) * V_ROWS, :] = ones

    def write_f32():
        if n_prev:
            kf_ref[0:n_prev] = pk_ref[...]
            vf_ref[0:n_prev] = pv_ref[...]
        kf_ref[n_prev] = kt_scr[...]
        if vf_layout == "heads":
            v = vt.T
            vf_new = vf_ref.at[n_prev]
            for h in range(d // LANES):
                vf_new[pl.ds(h, tm, stride=d // LANES), :] = v[:, _head(h)]
        else:
            vf_ref[n_prev] = vt

    if keep_t < n_t:
        pl.when(pl.program_id(0) % n_t >= n_t - keep_t)(write_f32)
    else:
        write_f32()


def _project_prompt(x, g, w_in_t, q_gain, k_gain, *, batch, seq, keep, rope_tables=None, vf_layout="cols",
                    prev_k=None, prev_v=None):
    rows, d = x.shape
    n_heads = d // LANES
    tm = _row_tile(math.gcd(seq, keep))
    n_t, keep_t = seq // tm, keep // tm
    att_t = min(ATT_TILE, tm)
    n_prev = 0 if prev_k is None else prev_k.shape[0]
    n = n_prev + 1

    kept_tile = lambda i: jnp.maximum(i % n_t - (n_t - keep_t), 0)
    cols_spec = lambda m: pl.BlockSpec((m, None, d, tm), lambda i: (0, i // n_t, 0, kept_tile(i)))
    heads_spec = lambda m: pl.BlockSpec((m, tm * n_heads, LANES),
                                        lambda i: (0, (i // n_t) * keep_t + kept_tile(i), 0))
    v_f32_spec = heads_spec if vf_layout == "heads" else cols_spec
    v_f32_shape = (n, batch * keep * n_heads, LANES) if vf_layout == "heads" else (n, batch, d, keep)

    row_spec = pl.BlockSpec((tm, d), lambda i: (i, 0))
    in_specs = [row_spec, _const_spec((1, d)), _const_spec((3 * d, d)), _const_spec((d, tm)), _const_spec((d, tm))]
    args = [x, g, w_in_t, q_gain, k_gain]
    if rope_tables is not None:
        cos, sin = rope_tables
        n_rt = cos.shape[1] // tm
        in_specs += [pl.BlockSpec((GROUP // 2, tm), lambda i: (0, i % n_rt))] * 2
        args += [cos, sin]
    if n_prev:
        in_specs += [cols_spec(n_prev), v_f32_spec(n_prev)]
        args += [prev_k, prev_v]
    return pl.pallas_call(
        functools.partial(_proj_prompt_kernel, rope=rope_tables is not None, vf_layout=vf_layout, n_prev=n_prev,
                          n_t=n_t, keep_t=keep_t, att_t=att_t),
        grid=(rows // tm,),
        in_specs=in_specs,
        out_specs=[pl.BlockSpec((None, d, tm), lambda i: (i // n_t, 0, i % n_t)), row_spec,
                   pl.BlockSpec((tm // att_t, n_heads * V_ROWS, att_t), lambda i: (i, 0, 0)), cols_spec(n),
                   v_f32_spec(n)],
        out_shape=[jax.ShapeDtypeStruct((batch, d, seq), BF16), jax.ShapeDtypeStruct((rows, d), BF16),
                   jax.ShapeDtypeStruct((rows // att_t, n_heads * V_ROWS, att_t), BF16),
                   jax.ShapeDtypeStruct((n, batch, d, keep), F32), jax.ShapeDtypeStruct(v_f32_shape, F32)],
        scratch_shapes=[pltpu.VMEM((d, tm), F32), pltpu.VMEM((d, tm), F32)],
        compiler_params=_params("arbitrary"),
        name="qkv_project_prompt",
    )(*args)


def _lambda_value(lp, lam_init):
    return (jnp.exp(jnp.sum(lp[0:1] * lp[1:2], keepdims=True))
            - jnp.exp(jnp.sum(lp[2:3] * lp[3:4], keepdims=True)) + lam_init)


def _reset_state(m_scr, l_scr, acc_scr):
    m_scr[...] = jnp.full(m_scr.shape, NEG, F32)
    l_scr[...] = jnp.zeros(l_scr.shape, F32)
    acc_scr[...] = jnp.zeros(acc_scr.shape, F32)


def _stack_groups_t(qt):
    feat = lax.broadcasted_iota(jnp.int32, qt.shape, 0)
    zero = jnp.zeros_like(qt)
    low = feat < GROUP
    return jnp.concatenate([jnp.where(low, qt, zero), jnp.where(low, zero, qt)], axis=1)


def _reset_state_t(m_scr, acc_scr):
    m_scr[...] = jnp.full(m_scr.shape, NEG, F32)
    acc_scr[...] = jnp.zeros(acc_scr.shape, F32)


def _update_state_t(m_scr, acc_scr, h, s, vt):
    m = m_scr[h]
    m_new = jnp.maximum(m, jnp.max(s, axis=0, keepdims=True))
    p = jnp.exp2(s - m_new)
    acc_scr[h] = jnp.exp2(m - m_new) * acc_scr[h] + jnp.dot(vt, p.astype(BF16), preferred_element_type=F32)
    m_scr[h] = m_new


def _normalised_t(acc):
    return acc[:LANES] * (1.0 / acc[LANES:LANES + 1])


def _state_scratch_t(n_heads, t):
    return [pltpu.VMEM((n_heads, 1, 2 * t), F32), pltpu.VMEM((n_heads, V_ROWS, 2 * t), F32)]


def _stack_groups(qh):
    lane = lax.broadcasted_iota(jnp.int32, qh.shape, 1)
    zero = jnp.zeros_like(qh)
    low = lane < GROUP
    return jnp.concatenate([jnp.where(low, qh, zero), jnp.where(low, zero, qh)], axis=0)


def _dot_nt(a, b):
    return lax.dot_general(a, b, (((1,), (1,)), ((), ())), preferred_element_type=F32)


def _update_state(m_scr, l_scr, acc_scr, h, s, pv):
    m = m_scr[h]
    m_new = jnp.maximum(m, jnp.max(s, axis=1, keepdims=True))
    alpha = jnp.exp2(m - m_new)
    p = jnp.exp2(s - m_new)
    l_scr[h] = alpha * l_scr[h] + jnp.sum(p, axis=1, keepdims=True)
    acc_scr[h] = alpha * acc_scr[h] + pv(p.astype(BF16))
    m_scr[h] = m_new


def _state_scratch(n_heads, t):
    return [pltpu.VMEM((n_heads, 2 * t, 1), F32), pltpu.VMEM((n_heads, 2 * t, 1), F32),
            pltpu.VMEM((n_heads, 2 * t, LANES), F32)]


def _diff_combine(on, t, lam, sub_gain, lam_init):
    o = on[:t] - lam * on[t:]
    return _rms(o, sub_gain) * (1.0 - lam_init)


def _select_groups(on, t):
    lane = lax.broadcasted_iota(jnp.int32, (t, LANES), 1)
    return jnp.where(lane < GROUP, on[:t], on[t:])


def _diff_prompt_kernel(qt_ref, k_ref, vt_ref, x_ref, wo_ref, sg_ref, lp_ref, o_ref,
                        qs_scr, m_scr, acc_scr, s_scr, mask_scr, obuf, *, lam_init):
    i = pl.program_id(1)
    d, t = qt_ref.shape
    n_heads = d // LANES

    @pl.when((pl.program_id(0) == 0) & (i == 0))
    def _():
        key = lax.broadcasted_iota(jnp.int32, (t, 2 * t), 0)
        qry = lax.broadcasted_iota(jnp.int32, (t, 2 * t), 1) % t
        mask_scr[...] = jnp.where((key // CHUNK) <= (qry // CHUNK), 0.0, NEG)

    for h in range(n_heads):
        qs_scr[h] = _stack_groups_t(qt_ref[_head(h), :])
    _reset_state_t(m_scr, acc_scr)

    def scores(j, slot, diagonal=False):
        rows = pl.ds(pl.multiple_of(j * t, t), t)
        for h in range(n_heads):
            s = jnp.dot(k_ref[rows, _head(h)], qs_scr[h], preferred_element_type=F32)
            s_scr[slot, h] = s + mask_scr[...] if diagonal else s

    def consume(j, slot):
        for h in range(n_heads):
            _update_state_t(m_scr, acc_scr, h, s_scr[slot, h], vt_ref[j, h * V_ROWS:(h + 1) * V_ROWS, :])

    pairs = jnp.maximum(i - 1, 0) // 2
    pl.when(i > 0)(lambda: scores(0, 0))

    def body(p, c):
        j = 2 * p
        scores(j + 1, 1)
        consume(j, 0)
        scores(j + 2, 0)
        consume(j + 1, 1)
        return c

    lax.fori_loop(0, pairs, body, 0)
    left = i - 2 * pairs

    @pl.when(left == 0)
    def _():
        scores(i, 0, True)
        consume(i, 0)

    @pl.when(left == 1)
    def _():
        scores(i, 1, True)
        consume(i - 1, 0)
        consume(i, 1)

    @pl.when(left == 2)
    def _():
        scores(i - 1, 1)
        consume(i - 2, 0)
        scores(i, 0, True)
        consume(i - 1, 1)
        consume(i, 0)

    lam = _lambda_value(lp_ref[...], lam_init)
    for h in range(n_heads):
        on = _normalised_t(acc_scr[h])
        o = on[:, :t] - lam * on[:, t:]
        o = o * lax.rsqrt(jnp.mean(o * o, axis=0, keepdims=True) + EPS)
        obuf[:, _head(h)] = (o.T * sg_ref[:, _head(h)] * (1.0 - lam_init)).astype(BF16)
    o_ref[...] = x_ref[...] + jnp.dot(obuf[...], wo_ref[...], preferred_element_type=F32)


def _diff_attention_prompt(qt, k, vt, x, w_out, sub_gain, lam_p, *, batch, seq, lam_init):
    rows, d = x.shape
    t = vt.shape[2]
    nq = seq // t
    n_heads = d // LANES
    row_spec = pl.BlockSpec((t, d), lambda b, i: (b * nq + i, 0))
    return pl.pallas_call(
        functools.partial(_diff_prompt_kernel, lam_init=lam_init),
        grid=(batch, nq),
        in_specs=[pl.BlockSpec((None, d, t), lambda b, i: (b, 0, i)),
                  pl.BlockSpec((seq, d), lambda b, i: (b, 0)),
                  pl.BlockSpec((nq, n_heads * V_ROWS, t), lambda b, i: (b, 0, 0)),
                  row_spec, _const_spec((d, d)), _const_spec((1, d)), _const_spec(lam_p.shape)],
        out_specs=row_spec,
        out_shape=jax.ShapeDtypeStruct((rows, d), F32),
        scratch_shapes=[pltpu.VMEM((n_heads, LANES, 2 * t), BF16)] + _state_scratch_t(n_heads, t)
        + [pltpu.VMEM((2, n_heads, t, 2 * t), F32), pltpu.VMEM((t, 2 * t), F32), pltpu.VMEM((t, d), BF16)],
        compiler_params=_params("arbitrary", "arbitrary"),
        name="diff_attention_prompt",
    )(qt, k, vt, x, w_out, sub_gain, lam_p)


def _diff_decode_kernel(q_ref, kc_ref, vc_ref, kn_ref, vn_ref, x_ref, wo_ref, sg_ref, lp_ref, o_ref,
                        m_scr, l_scr, acc_scr, s_scr, obuf, *, lam_init):
    j = pl.program_id(1)
    t, d = q_ref.shape
    n_heads = d // LANES
    tk = kc_ref.shape[1]

    @pl.when(j == 0)
    def _():
        _reset_state(m_scr, l_scr, acc_scr)

    for h in range(n_heads):
        qs = _stack_groups(q_ref[:, _head(h)])
        s_scr[h] = jnp.dot(qs, kc_ref[_head(h), :].astype(BF16), preferred_element_type=F32)
    for h in range(n_heads):
        vh = vc_ref[pl.ds(h, tk, stride=n_heads), :].astype(BF16)
        _update_state(m_scr, l_scr, acc_scr, h, s_scr[h], lambda p: jnp.dot(p, vh, preferred_element_type=F32))

    @pl.when(j == pl.num_programs(1) - 1)
    def _():
        lam = _lambda_value(lp_ref[...], lam_init)
        for h in range(n_heads):
            qs = _stack_groups(q_ref[:, _head(h)])
            vh = vn_ref[:, _head(h)]
            _update_state(m_scr, l_scr, acc_scr, h, _dot_nt(qs, kn_ref[:, _head(h)]),
                          lambda p: jnp.dot(p, vh, preferred_element_type=F32))
            on = acc_scr[h] / l_scr[h]
            obuf[:, _head(h)] = _diff_combine(on, t, lam, sg_ref[:, _head(h)], lam_init).astype(BF16)
        o_ref[...] = x_ref[...] + jnp.dot(obuf[...], wo_ref[...], preferred_element_type=F32)


def _diff_attention_decode(q, cache_kt, cache_vh, k_new, v_new, x, w_out, sub_gain, lam_p, *, layer, lam_init):
    _, batch, d, past = cache_kt.shape
    rows = x.shape[0]
    t = rows // batch
    tk = min(DEC_KEY_TILE, past)
    n_heads = d // LANES
    row_spec = pl.BlockSpec((t, d), lambda b, j: (b, 0))
    kt_spec = pl.BlockSpec((None, None, d, tk), lambda b, j: (layer, b, 0, j))
    v_spec = pl.BlockSpec((None, None, tk * n_heads, LANES), lambda b, j: (layer, b, j, 0))
    return pl.pallas_call(
        functools.partial(_diff_decode_kernel, lam_init=lam_init),
        grid=(batch, past // tk),
        in_specs=[row_spec, kt_spec, v_spec, row_spec, row_spec, row_spec, _const_spec((d, d)),
                  _const_spec((1, d)), _const_spec(lam_p.shape)],
        out_specs=row_spec,
        out_shape=jax.ShapeDtypeStruct((rows, d), F32),
        scratch_shapes=_state_scratch(n_heads, t) + [pltpu.VMEM((n_heads, 2 * t, tk), F32), pltpu.VMEM((t, d), BF16)],
        compiler_params=_params("arbitrary", "arbitrary"),
        name="diff_attention_decode",
    )(q, cache_kt, cache_vh, k_new, v_new, x, w_out, sub_gain, lam_p)


def _gather_rows(tbl, idx_of_col, width):
    n_pad = tbl.shape[1]
    t_idx = lax.broadcasted_iota(jnp.int32, (n_pad, width), 0)
    v_idx = lax.broadcasted_iota(jnp.int32, (n_pad, width), 1)
    onehot = jnp.where(t_idx == idx_of_col(v_idx), 1.0, 0.0).astype(BF16)
    hi = tbl.astype(BF16)
    r1 = tbl - hi.astype(F32)
    mid = r1.astype(BF16)
    lo = (r1 - mid.astype(F32)).astype(BF16)
    dot = lambda a: jnp.dot(a, onehot, preferred_element_type=F32)
    return dot(hi) + dot(mid) + dot(lo)


def _rel_index(rel):
    return jnp.clip(rel, -C_REL_CLIP, C_REL_CLIP) + C_REL_CLIP


def _toeplitz(vec, rows, cols, origin):
    width = vec.shape[1]
    xb = jnp.broadcast_to(vec, (rows, width))
    return pltpu.roll(xb, (width - origin) % width, 1, stride=1, stride_axis=0)[:, :cols]


def _bias_kernel(tbl_ref, bp_ref, bs_ref, *, t, n_blocks, ts, win):
    tbl = tbl_ref[...] * LOG2E
    heads = tbl.shape[0]
    key = lax.broadcasted_iota(jnp.int32, (t, t), 0)
    qry = lax.broadcasted_iota(jnp.int32, (t, t), 1)
    for jb in range(n_blocks):
        off = (n_blocks - 1 - jb) * t
        g = _gather_rows(tbl, lambda v: _rel_index(v - (t - 1) + off), 2 * t)
        key_chunk = (jb * t + key) // CHUNK - (n_blocks - 1) * t // CHUNK
        q_chunk = qry // CHUNK
        visible = (key_chunk <= q_chunk) & (key_chunk >= q_chunk - C_LEFT_CHUNKS)
        for h in range(heads):
            tile = _toeplitz(g[h:h + 1, :], t, t, t - 1)
            bp_ref[jb, h // 2, :, (h % 2) * t:(h % 2 + 1) * t] = jnp.where(visible, tile, NEG)
    width = bs_ref.shape[-1]
    wide = 1 << (width + ts).bit_length()
    g = _gather_rows(tbl, lambda v: _rel_index(win + ts - 1 - v), wide)
    for h in range(heads):
        tile = _toeplitz(g[h:h + 1, :], ts, width, ts - 1)
        bs_ref[h // 2, (h % 2) * ts:(h % 2 + 1) * ts, :] = tile


def _band_bias(table_t, *, t, n_blocks, ts, win):
    heads, n = table_t.shape
    n_pad = -(-n // LANES) * LANES
    tbl = jnp.pad(table_t, ((0, 0), (0, n_pad - n)))
    return pl.pallas_call(
        functools.partial(_bias_kernel, t=t, n_blocks=n_blocks, ts=ts, win=win),
        out_shape=[jax.ShapeDtypeStruct((n_blocks, heads // 2, t, 2 * t), F32),
                   jax.ShapeDtypeStruct((heads // 2, 2 * ts, win + ts), F32)],
        compiler_params=pltpu.CompilerParams(vmem_limit_bytes=VMEM_LIMIT_BYTES),
        name="band_bias",
    )(tbl)


def _band_prompt_kernel(qt_ref, k_ref, vt_ref, b_ref, x_ref, wo_ref, o_ref, qs_scr, m_scr, acc_scr, s_scr, obuf):
    i = pl.program_id(1)
    d, t = qt_ref.shape
    n_heads = d // LANES
    n_blocks = b_ref.shape[0]
    first = jnp.maximum(n_blocks - 1 - i, 0)
    for h in range(n_heads):
        qs_scr[h] = _stack_groups_t(qt_ref[_head(h), :])
    _reset_state_t(m_scr, acc_scr)

    def scores(jb):
        rows = pl.ds(pl.multiple_of((i - (n_blocks - 1) + jb) * t, t), t)
        for h in range(n_heads):
            s_scr[jb % 2, h] = (jnp.dot(k_ref[rows, _head(h)], qs_scr[h], preferred_element_type=F32)
                                + b_ref[jb, h])

    def consume(jb):
        j = i - (n_blocks - 1) + jb
        for h in range(n_heads):
            _update_state_t(m_scr, acc_scr, h, s_scr[jb % 2, h], vt_ref[j, h * V_ROWS:(h + 1) * V_ROWS, :])

    def window(f):
        scores(f)
        for jb in range(f, n_blocks):
            if jb + 1 < n_blocks:
                scores(jb + 1)
            consume(jb)

    for f in range(n_blocks):
        pl.when(first == f)(functools.partial(window, f))
    feat = lax.broadcasted_iota(jnp.int32, (LANES, t), 0)
    for h in range(n_heads):
        on = _normalised_t(acc_scr[h])
        obuf[:, _head(h)] = jnp.where(feat < GROUP, on[:, :t], on[:, t:]).T.astype(BF16)
    o_ref[...] = x_ref[...] + jnp.dot(obuf[...], wo_ref[...], preferred_element_type=F32)


def _band_attention_prompt(qt, k, vt, bias, x, w_out, *, batch, seq):
    rows, d = x.shape
    t = vt.shape[2]
    nq = seq // t
    n_heads = d // LANES
    row_spec = pl.BlockSpec((t, d), lambda b, i: (b * nq + i, 0))
    return pl.pallas_call(
        _band_prompt_kernel,
        grid=(batch, nq),
        in_specs=[pl.BlockSpec((None, d, t), lambda b, i: (b, 0, i)),
                  pl.BlockSpec((seq, d), lambda b, i: (b, 0)),
                  pl.BlockSpec((nq, n_heads * V_ROWS, t), lambda b, i: (b, 0, 0)),
                  _const_spec(bias.shape), row_spec, _const_spec((d, d))],
        out_specs=row_spec,
        out_shape=jax.ShapeDtypeStruct((rows, d), F32),
        scratch_shapes=[pltpu.VMEM((n_heads, LANES, 2 * t), BF16)] + _state_scratch_t(n_heads, t)
        + [pltpu.VMEM((2, n_heads, t, 2 * t), F32), pltpu.VMEM((t, d), BF16)],
        compiler_params=_params("arbitrary", "arbitrary"),
        name="band_attention_prompt",
    )(qt, k, vt, bias, x, w_out)


def _band_decode_kernel(q_ref, kc_ref, vc_ref, kn_ref, vn_ref, b_ref, x_ref, wo_ref, o_ref, obuf):
    t, d = q_ref.shape
    win = kc_ref.shape[1]
    for h in range(d // LANES):
        qs = _stack_groups(q_ref[:, _head(h)])
        bias = b_ref[h]
        s_c = jnp.dot(qs, kc_ref[_head(h), :].astype(BF16), preferred_element_type=F32) + bias[:, :win]
        s_n = _dot_nt(qs, kn_ref[:, _head(h)]) + bias[:, win:]
        m = jnp.maximum(jnp.max(s_c, axis=1, keepdims=True), jnp.max(s_n, axis=1, keepdims=True))
        p_c = jnp.exp2(s_c - m)
        p_n = jnp.exp2(s_n - m)
        l = jnp.sum(p_c, axis=1, keepdims=True) + jnp.sum(p_n, axis=1, keepdims=True)
        acc = (_dot_nt(p_c.astype(BF16), vc_ref[_head(h), :].astype(BF16))
               + jnp.dot(p_n.astype(BF16), vn_ref[:, _head(h)], preferred_element_type=F32))
        obuf[:, _head(h)] = _select_groups(acc / l, t).astype(BF16)
    o_ref[...] = x_ref[...] + jnp.dot(obuf[...], wo_ref[...], preferred_element_type=F32)


def _band_attention_decode(q, cache_kt, cache_vt, k_new, v_new, bias, x, w_out, *, layer):
    _, batch, d, win = cache_kt.shape
    rows = x.shape[0]
    t = rows // batch
    row_spec = pl.BlockSpec((t, d), lambda b: (b, 0))
    cache_spec = pl.BlockSpec((None, None, d, win), lambda b: (layer, b, 0, 0))
    return pl.pallas_call(
        _band_decode_kernel,
        grid=(batch,),
        in_specs=[row_spec, cache_spec, cache_spec, row_spec, row_spec, _const_spec(bias.shape), row_spec,
                  _const_spec((d, d))],
        out_specs=row_spec,
        out_shape=jax.ShapeDtypeStruct((rows, d), F32),
        scratch_shapes=[pltpu.VMEM((t, d), BF16)],
        compiler_params=_params("arbitrary"),
        name="band_attention_decode",
    )(q, cache_kt, cache_vt, k_new, v_new, bias, x, w_out)


def _gelu_tanh(x):
    c = -2.0 * math.sqrt(2.0 / math.pi) * LOG2E
    return x / (1.0 + jnp.exp2(x * (c + (0.044715 * c) * (x * x))))


def _rglru_kernel(x_ref, g_ref, win_ref, bin_ref, cw_ref, cb_ref, gaw_ref, gab_ref, gxw_ref, gxb_ref,
                  lam_ref, wout_ref, hist_ref, h0_ref, o_ref, cst_ref, hst_ref, ubuf, hcar, h_scr):
    tm, w = h_scr.shape
    pad = SUBLANES

    @pl.when(pl.program_id(1) == 0)
    def _():
        ubuf[...] = hist_ref[...]
        hcar[...] = h0_ref[...]

    x = x_ref[...]
    xn = _rms(x, g_ref[...]).astype(BF16)
    gu = jnp.dot(xn, win_ref[...], preferred_element_type=F32) + bin_ref[...]
    gate = _gelu_tanh(gu[:, :w])
    u = gu[:, w:]
    cw = cw_ref[...]
    prev = ubuf[...]
    sub8 = lax.broadcasted_iota(jnp.int32, (pad, w), 0)
    xc = cb_ref[...] + cw[3:4] * u
    for j in range(B_CONV_W - 1):
        back = B_CONV_W - 1 - j
        rolled = pltpu.roll(u, back, 0)
        head = jnp.where(sub8 < back, pltpu.roll(prev, back, 0), rolled[0:pad])
        xc = xc + cw[j:j + 1] * jnp.concatenate([head, rolled[pad:]], axis=0)

    xcb = xc.astype(BF16)
    bw = w // B_BLOCKS
    r_parts, i_parts = [], []
    for n in range(B_BLOCKS):
        blk = xcb[:, n * bw:(n + 1) * bw]
        r_parts.append(jnp.dot(blk, gaw_ref[n], preferred_element_type=F32))
        i_parts.append(jnp.dot(blk, gxw_ref[n], preferred_element_type=F32))
    r = jax.nn.sigmoid(jnp.concatenate(r_parts, axis=1) + gab_ref[...])
    ig = jax.nn.sigmoid(jnp.concatenate(i_parts, axis=1) + gxb_ref[...])
    z = -lam_ref[...]
    softplus = jnp.maximum(z, 0.0) + jnp.log1p(jnp.exp(-jnp.abs(z)))
    a = jnp.exp(-B_C * r * softplus)
    b = jnp.sqrt(1.0 - a * a) * (ig * xc)

    sub = lax.broadcasted_iota(jnp.int32, (tm, w), 0) % SUBLANES
    def group_roll(v, s):
        return pltpu.roll(v.reshape(tm // SUBLANES, SUBLANES, w), s, 1).reshape(tm, w)

    for s in (1, 2, 4):
        a_prev = group_roll(a, s)
        b_prev = group_roll(b, s)
        inside = sub >= s
        b = jnp.where(inside, a * b_prev + b, b)
        a = jnp.where(inside, a * a_prev, a)
    h_prev = hcar[...]
    for gidx in range(tm // SUBLANES):
        rows = slice(gidx * SUBLANES, (gidx + 1) * SUBLANES)
        hg = a[rows] * h_prev + b[rows]
        h_scr[rows, :] = hg
        h_prev = jnp.broadcast_to(hg[SUBLANES - 1:SUBLANES, :], (SUBLANES, w))
    hcar[...] = h_prev
    hs = h_scr[...]

    y = jnp.dot((hs * gate).astype(BF16), wout_ref[...], preferred_element_type=F32)
    o_ref[...] = x + y
    ubuf[...] = u[tm - pad:tm, :]
    cst_ref[...] = u[tm - pad:tm, :]
    hst_ref[...] = h_prev


def _rglru(x, g, p, hist, h0, *, batch, seq):
    rows, d = x.shape
    w = p["w_out"].shape[0]
    tm = _row_tile(seq)
    nt = seq // tm
    row_spec = pl.BlockSpec((tm, d), lambda b, t: (b * nt + t, 0))
    state_spec = pl.BlockSpec((None, SUBLANES, w), lambda b, t: (b, 0, 0))
    consts = [g, p["w_in"], p["b_in"], p["conv_w"], p["conv_b"], p["ga_w"], p["ga_b"], p["gx_w"], p["gx_b"],
              p["lam"], p["w_out"]]
    return pl.pallas_call(
        _rglru_kernel,
        grid=(batch, nt),
        in_specs=[row_spec] + [_const_spec(c.shape) for c in consts] + [state_spec, state_spec],
        out_specs=[row_spec, state_spec, state_spec],
        out_shape=[jax.ShapeDtypeStruct((rows, d), F32), jax.ShapeDtypeStruct((batch, SUBLANES, w), F32),
                   jax.ShapeDtypeStruct((batch, SUBLANES, w), F32)],
        scratch_shapes=[pltpu.VMEM((SUBLANES, w), F32), pltpu.VMEM((SUBLANES, w), F32), pltpu.VMEM((tm, w), F32)],
        compiler_params=_params("arbitrary", "arbitrary"),
        name="rglru_block",
    )(x, *consts, hist, h0)


def _mlp_kernel(x_ref, g_ref, w1_ref, w2_ref, gf_ref, o_ref, *, final_norm):
    x = x_ref[...]
    xn = _rms(x, g_ref[...]).astype(BF16)
    acc = x
    for c in range(w1_ref.shape[1] // FF_CHUNK):
        cols = slice(c * FF_CHUNK, (c + 1) * FF_CHUNK)
        h = jnp.dot(xn, w1_ref[:, cols], preferred_element_type=F32)
        h = jnp.square(jnp.maximum(h, 0.0)).astype(BF16)
        acc = acc + jnp.dot(h, w2_ref[cols, :], preferred_element_type=F32)
    if final_norm:
        acc = _rms(acc, gf_ref[...])
    o_ref[...] = acc


def _mlp(x, g, w1, w2, g_final, *, final_norm):
    rows, d = x.shape
    tm = _row_tile(rows, MLP_TILE)
    row_spec = pl.BlockSpec((tm, d), lambda i: (i, 0))
    return pl.pallas_call(
        functools.partial(_mlp_kernel, final_norm=final_norm),
        grid=(rows // tm,),
        in_specs=[row_spec, _const_spec((1, d)), _const_spec(w1.shape), _const_spec(w2.shape),
                  _const_spec((1, d))],
        out_specs=row_spec,
        out_shape=jax.ShapeDtypeStruct((rows, d), F32),
        compiler_params=_params("arbitrary"),
        name="mlp",
    )(x, g, w1, w2, g_final)


def _rope_angles(pos):
    half = GROUP // 2
    inv = ROPE_THETA ** (-jnp.arange(half, dtype=F32) / half)
    return pos.astype(F32)[:, None] * inv[None, :]


def _rope_tables_rows(pos):
    ang = _rope_angles(pos)
    cos = jnp.tile(jnp.cos(ang), (1, LANES // ang.shape[1]))
    sin = jnp.tile(jnp.concatenate([-jnp.sin(ang), jnp.sin(ang)], axis=1), (1, LANES // GROUP))
    return cos, sin


def _rope_tables_cols(pos):
    ang = _rope_angles(pos).T
    return jnp.cos(ang), jnp.sin(ang)


def _feature_major(cache):
    n, batch, pos = cache.shape[:3]
    nd = cache.ndim
    return jnp.transpose(cache, (0, 1) + tuple(range(3, nd)) + (2,)).reshape(n, batch, -1, pos)


def _position_major(x, feature_dims):
    n, batch, _, pos = x.shape
    nf = len(feature_dims)
    return jnp.transpose(x.reshape(n, batch, *feature_dims, pos), (0, 1, 2 + nf) + tuple(range(2, 2 + nf)))


def kernel(x_prompt, x_sample, cache_a_k, cache_a_v, state_b_conv, state_b_h, cache_c_k, cache_c_v, norm_mix_g, norm_mlp_g, norm_final_g, a_w_in, a_q_norm_g, a_k_norm_g, a_lambda, a_subln_g, a_w_out, b_w_in, b_b_in, b_conv_w, b_conv_b, b_gate_a_w, b_gate_a_b, b_gate_x_w, b_gate_x_b, b_lambda, b_w_out, c_w_in, c_q_norm_g, c_k_norm_g, c_rel_bias, c_w_out, mlp_w1, mlp_w2):
    batch, seq, d = x_prompt.shape
    dbatch, dseq, _ = x_sample.shape
    depth = norm_mix_g.shape[0]
    past = cache_a_k.shape[2]
    n_heads = d // LANES
    xp = x_prompt.reshape(batch * seq, d)
    xs = x_sample.reshape(dbatch * dseq, d)
    row = lambda v: v.reshape(1, -1).astype(F32)
    tile_gain = lambda v: jnp.tile(v.astype(F32), d // v.shape[0]).reshape(1, d)
    grp = jnp.arange(LANES) // GROUP
    bd = (grp[:, None] == grp[None, :]).astype(BF16)
    col_gain = lambda v: jnp.broadcast_to(tile_gain(v).reshape(d, 1), (d, min(ROW_TILE, seq)))
    rope_p = _rope_tables_cols(jnp.arange(seq))
    rope_s = tuple(jnp.tile(t, (dbatch, 1)) for t in _rope_tables_rows(past + jnp.arange(dseq)))
    keep = min(C_BAND_PAST, seq)
    att_t = min(ATT_TILE, seq)
    n_band_blocks = C_BAND_PAST // att_t + 1
    a_kt = _feature_major(cache_a_k)
    a_vh = cache_a_v.reshape(cache_a_v.shape[0], dbatch, past * n_heads, LANES)
    c_kt, c_vt = _feature_major(cache_c_k), _feature_major(cache_c_v)
    c_heads = c_rel_bias.shape[2]

    outs = {name: [] for name in ("aks", "avs", "bcp", "bhp", "bcs", "bhs", "cks", "cvs")}
    akp = avp = ckp = cvp = None
    for layer in range(depth):
        kind, idx = layer % 3, layer // 3
        g_mix = row(norm_mix_g[layer])
        if kind == 0:
            lam_init = 0.8 - 0.6 * math.exp(-0.3 * layer)
            w_in = a_w_in[idx].astype(BF16)
            w_out = a_w_out[idx].astype(BF16)
            qg, kg, sg = tile_gain(a_q_norm_g[idx]), tile_gain(a_k_norm_g[idx]), tile_gain(a_subln_g[idx])
            lam_p = a_lambda[idx].astype(F32)
            qt, k, vt, akp, avp = _project_prompt(xp, g_mix, w_in.T, col_gain(a_q_norm_g[idx]),
                                                  col_gain(a_k_norm_g[idx]), batch=batch, seq=seq, keep=seq,
                                                  rope_tables=rope_p, vf_layout="heads", prev_k=akp, prev_v=avp)
            xp = _diff_attention_prompt(qt, k, vt, xp, w_out, sg, lam_p, batch=batch, seq=seq, lam_init=lam_init)
            q, k, v, kf, vf = _project_rows(xs, g_mix, w_in, qg, kg, bd, rope_tables=rope_s)
            xs = _diff_attention_decode(q, a_kt, a_vh, k, v, xs, w_out, sg, lam_p, layer=idx, lam_init=lam_init)
            outs["aks"].append(kf.reshape(dbatch, dseq, n_heads, 2, GROUP))
            outs["avs"].append(vf.reshape(dbatch, dseq, n_heads, LANES))
        elif kind == 1:
            w = b_w_out.shape[1]
            p = dict(w_in=b_w_in[idx].astype(BF16), b_in=row(b_b_in[idx]), conv_w=b_conv_w[idx].astype(F32),
                     conv_b=row(b_conv_b[idx]), ga_w=b_gate_a_w[idx].astype(BF16), ga_b=row(b_gate_a_b[idx]),
                     gx_w=b_gate_x_w[idx].astype(BF16), gx_b=row(b_gate_x_b[idx]), lam=row(b_lambda[idx]),
                     w_out=b_w_out[idx].astype(BF16))
            zeros = jnp.zeros((batch, SUBLANES, w), F32)
            xp, cst, hst = _rglru(xp, g_mix, p, zeros, zeros, batch=batch, seq=seq)
            outs["bcp"].append(cst[:, SUBLANES - (B_CONV_W - 1):])
            outs["bhp"].append(hst[:, 0])
            hist = jnp.pad(state_b_conv[idx].astype(F32), ((0, 0), (SUBLANES - (B_CONV_W - 1), 0), (0, 0)))
            h0 = jnp.broadcast_to(state_b_h[idx].astype(F32)[:, None, :], (dbatch, SUBLANES, w))
            xs, cst, hst = _rglru(xs, g_mix, p, hist, h0, batch=dbatch, seq=dseq)
            outs["bcs"].append(cst[:, SUBLANES - (B_CONV_W - 1):])
            outs["bhs"].append(hst[:, 0])
        else:
            w_in = c_w_in[idx].astype(BF16)
            w_out = c_w_out[idx].astype(BF16)
            qg, kg = tile_gain(c_q_norm_g[idx]), tile_gain(c_k_norm_g[idx])
            win = c_kt.shape[3]
            bias_p, bias_s = _band_bias(c_rel_bias[idx].astype(F32).T, t=att_t, n_blocks=n_band_blocks, ts=dseq,
                                        win=win)
            qt, k, vt, ckp, cvp = _project_prompt(xp, g_mix, w_in.T, col_gain(c_q_norm_g[idx]),
                                                  col_gain(c_k_norm_g[idx]), batch=batch, seq=seq, keep=keep,
                                                  prev_k=ckp, prev_v=cvp)
            xp = _band_attention_prompt(qt, k, vt, bias_p, xp, w_out, batch=batch, seq=seq)
            q, k, v, kf, vf = _project_rows(xs, g_mix, w_in, qg, kg, bd)
            xs = _band_attention_decode(q, c_kt, c_vt, k, v, bias_s, xs, w_out, layer=idx)
            outs["cks"].append(kf.reshape(dbatch, dseq, c_heads, GROUP))
            outs["cvs"].append(vf.reshape(dbatch, dseq, c_heads, GROUP))
        g_mlp = row(norm_mlp_g[layer])
        w1 = mlp_w1[layer].astype(BF16)
        w2 = mlp_w2[layer].astype(BF16)
        last = layer == depth - 1
        xp = _mlp(xp, g_mlp, w1, w2, row(norm_final_g), final_norm=last)
        xs = _mlp(xs, g_mlp, w1, w2, row(norm_final_g), final_norm=last)
    st = lambda name: jnp.stack(outs[name])
    return (xp.reshape(batch, seq, d), xs.reshape(dbatch, dseq, d),
            _position_major(akp, (n_heads, 2, GROUP)), avp.reshape(-1, batch, seq, n_heads, LANES),
            st("aks"), st("avs"),
            st("bcp"), st("bhp"), st("bcs"), st("bhs"),
            _position_major(ckp, (c_heads, GROUP)), _position_major(cvp, (c_heads, GROUP)),
            st("cks"), st("cvs"))
```

```python
import functools
import math

import jax
import jax.numpy as jnp
from jax import lax
from jax.experimental import pallas as pl
from jax.experimental.pallas import tpu as pltpu

F32 = jnp.float32
BF16 = jnp.bfloat16

EPS = 1e-6
CHUNK = 64
GROUP = 64
LANES = 128
SUBLANES = 8
ROPE_THETA = 10000.0
LOG2E = 1.0 / math.log(2.0)
Q_SCALE = GROUP ** -0.5 * LOG2E
B_C = 8.0
B_CONV_W = 4
B_BLOCKS = 4
C_LEFT_CHUNKS = 8
C_BAND_PAST = C_LEFT_CHUNKS * CHUNK
C_REL_CLIP = 128
NEG = -1e30
VMEM_LIMIT_BYTES = 56 * 1024 * 1024

ROW_TILE = 512
MLP_TILE = 1024
ATT_TILE = 256
DEC_KEY_TILE = 2048
FF_CHUNK = 512
PROJ_ROWS = 256
ONES_ROWS = 16
V_ROWS = LANES + ONES_ROWS


def _row_tile(rows, target=ROW_TILE):
    t = min(rows, target)
    assert rows % t == 0, (rows, t)
    return t


def _params(*semantics):
    return pltpu.CompilerParams(dimension_semantics=semantics, vmem_limit_bytes=VMEM_LIMIT_BYTES)


def _const_spec(shape):
    nd = len(shape)
    return pl.BlockSpec(shape, lambda *_: (0,) * nd, pipeline_mode=pl.Buffered(1))


def _rms(x, g):
    ms = jnp.mean(x * x, axis=-1, keepdims=True)
    return x * lax.rsqrt(ms + EPS) * g


def _head(h):
    return slice(h * LANES, (h + 1) * LANES)


def _proj_rows_kernel(*refs, rope):
    it = iter(refs)
    x_ref, g_ref, w_ref, qg_ref, kg_ref, bd_ref = (next(it) for _ in range(6))
    cos_ref, sin_ref = (next(it), next(it)) if rope else (None, None)
    q_ref, k_ref, v_ref, kf_ref, vf_ref = (next(it) for _ in range(5))
    x = x_ref[...]
    tm, d = x.shape
    xn = _rms(x, g_ref[...]).astype(BF16)
    qkv = jnp.dot(xn, w_ref[...], preferred_element_type=F32)
    bd = bd_ref[...]

    def group_norm(t, gain):
        sq = (t * t).astype(BF16)
        ss = jnp.concatenate([jnp.dot(sq[:, _head(j)], bd, preferred_element_type=F32)
                              for j in range(d // LANES)], axis=1)
        return t * lax.rsqrt(ss * (1.0 / GROUP) + EPS) * gain

    def rotary(t):
        cos = cos_ref[...]
        sin = sin_ref[...]
        lane = lax.broadcasted_iota(jnp.int32, (tm, LANES), 1)
        first_half = (lane % GROUP) < (GROUP // 2)
        outs = []
        for j in range(d // LANES):
            tj = t[:, _head(j)]
            partner = jnp.where(first_half, pltpu.roll(tj, LANES - GROUP // 2, 1),
                                pltpu.roll(tj, GROUP // 2, 1))
            outs.append(tj * cos + partner * sin)
        return jnp.concatenate(outs, axis=1)

    q = group_norm(qkv[:, :d], qg_ref[...])
    k = group_norm(qkv[:, d:2 * d], kg_ref[...])
    v = qkv[:, 2 * d:]
    if rope:
        q = rotary(q)
        k = rotary(k)
    q_ref[...] = (q * Q_SCALE).astype(BF16)
    k_ref[...] = k.astype(BF16)
    v_ref[...] = v.astype(BF16)
    kf_ref[...] = k
    vf_ref[...] = v


def _project_rows(x, g, w_in, q_gain, k_gain, bd, rope_tables=None):
    rows, d = x.shape
    tm = _row_tile(rows)
    row_spec = pl.BlockSpec((tm, d), lambda i: (i, 0))
    in_specs = [row_spec, _const_spec((1, d)), _const_spec((d, 3 * d)), _const_spec((1, d)),
                _const_spec((1, d)), _const_spec((LANES, LANES))]
    args = [x, g, w_in, q_gain, k_gain, bd]
    if rope_tables is not None:
        in_specs += [pl.BlockSpec((tm, LANES), lambda i: (i, 0))] * 2
        args += list(rope_tables)
    return pl.pallas_call(
        functools.partial(_proj_rows_kernel, rope=rope_tables is not None),
        grid=(rows // tm,),
        in_specs=in_specs,
        out_specs=[row_spec] * 5,
        out_shape=[jax.ShapeDtypeStruct((rows, d), BF16)] * 3 + [jax.ShapeDtypeStruct((rows, d), F32)] * 2,
        compiler_params=_params("arbitrary"),
        name="qkv_project_rows",
    )(*args)


def _proj_prompt_kernel(*refs, rope, vf_layout, n_prev, n_t, keep_t, att_t):
    it = iter(refs)
    x_ref, g_ref, wt_ref, qg_ref, kg_ref = (next(it) for _ in range(5))
    cos_ref, sin_ref = (next(it), next(it)) if rope else (None, None)
    pk_ref, pv_ref = (next(it), next(it)) if n_prev else (None, None)
    qt_ref, k_ref, vt_ref, kf_ref, vf_ref, kt_scr, vt_scr = (next(it) for _ in range(7))
    x = x_ref[...]
    tm, d = x.shape
    xn = _rms(x, g_ref[...]).astype(BF16)
    half = GROUP // 2

    def project(first_row, n_rows):
        return lax.dot_general(wt_ref[first_row:first_row + n_rows, :], xn, (((1,), (1,)), ((), ())),
                               preferred_element_type=F32)

    def norm_rope(t, gain):
        ss = jnp.sum(t * t, axis=0, keepdims=True)
        tn = t * lax.rsqrt(ss * (1.0 / GROUP) + EPS) * gain
        if rope:
            cos, sin = cos_ref[...], sin_ref[...]
            x1, x2 = tn[:half], tn[half:]
            tn = jnp.concatenate([x1 * cos - x2 * sin, x2 * cos + x1 * sin], axis=0)
        return tn

    for c0 in range(0, d, PROJ_ROWS):
        q_c, k_c = project(c0, PROJ_ROWS), project(d + c0, PROJ_ROWS)
        vt_scr[c0:c0 + PROJ_ROWS, :] = project(2 * d + c0, PROJ_ROWS)
        for gi in range(PROJ_ROWS // GROUP):
            loc = slice(gi * GROUP, (gi + 1) * GROUP)
            rows = slice(c0 + gi * GROUP, c0 + (gi + 1) * GROUP)
            qt_ref[rows, :] = (norm_rope(q_c[loc], qg_ref[rows, :]) * Q_SCALE).astype(BF16)
            kt_scr[rows, :] = norm_rope(k_c[loc], kg_ref[rows, :])
    k_ref[...] = kt_scr[...].T.astype(BF16)
    vt = vt_scr[...]
    ones = jnp.ones((ONES_ROWS, att_t), BF16)
    for c in range(tm // att_t):
        for h in range(d // LANES):
            vt_ref[c, h * V_ROWS:h * V_ROWS + LANES, :] = vt[_head(h), c * att_t:(c + 1) * att_t].astype(BF16)
            vt_ref[c, h * V_ROWS + LANES:(h + 1) * V_ROWS, :] = ones

    def write_f32():
        if n_prev:
            kf_ref[0:n_prev] = pk_ref[...]
            vf_ref[0:n_prev] = pv_ref[...]
        kf_ref[n_prev] = kt_scr[...]
        if vf_layout == "heads":
            v = vt.T
            vf_new = vf_ref.at[n_prev]
            for h in range(d // LANES):
                vf_new[pl.ds(h, tm, stride=d // LANES), :] = v[:, _head(h)]
        else:
            vf_ref[n_prev] = vt

    if keep_t < n_t:
        pl.when(pl.program_id(0) % n_t >= n_t - keep_t)(write_f32)
    else:
        write_f32()


def _project_prompt(x, g, w_in_t, q_gain, k_gain, *, batch, seq, keep, rope_tables=None, vf_layout="cols",
                    prev_k=None, prev_v=None):
    rows, d = x.shape
    n_heads = d // LANES
    tm = _row_tile(math.gcd(seq, keep))
    n_t, keep_t = seq // tm, keep // tm
    att_t = min(ATT_TILE, tm)
    n_prev = 0 if prev_k is None else prev_k.shape[0]
    n = n_prev + 1

    kept_tile = lambda i: jnp.maximum(i % n_t - (n_t - keep_t), 0)
    cols_spec = lambda m: pl.BlockSpec((m, None, d, tm), lambda i: (0, i // n_t, 0, kept_tile(i)))
    heads_spec = lambda m: pl.BlockSpec((m, tm * n_heads, LANES),
                                        lambda i: (0, (i // n_t) * keep_t + kept_tile(i), 0))
    v_f32_spec = heads_spec if vf_layout == "heads" else cols_spec
    v_f32_shape = (n, batch * keep * n_heads, LANES) if vf_layout == "heads" else (n, batch, d, keep)

    row_spec = pl.BlockSpec((tm, d), lambda i: (i, 0))
    in_specs = [row_spec, _const_spec((1, d)), _const_spec((3 * d, d)), _const_spec((d, tm)), _const_spec((d, tm))]
    args = [x, g, w_in_t, q_gain, k_gain]
    if rope_tables is not None:
        cos, sin = rope_tables
        n_rt = cos.shape[1] // tm
        in_specs += [pl.BlockSpec((GROUP // 2, tm), lambda i: (0, i % n_rt))] * 2
        args += [cos, sin]
    if n_prev:
        in_specs += [cols_spec(n_prev), v_f32_spec(n_prev)]
        args += [prev_k, prev_v]
    return pl.pallas_call(
        functools.partial(_proj_prompt_kernel, rope=rope_tables is not None, vf_layout=vf_layout, n_prev=n_prev,
                          n_t=n_t, keep_t=keep_t, att_t=att_t),
        grid=(rows // tm,),
        in_specs=in_specs,
        out_specs=[pl.BlockSpec((None, d, tm), lambda i: (i // n_t, 0, i % n_t)), row_spec,
                   pl.BlockSpec((tm // att_t, n_heads * V_ROWS, att_t), lambda i: (i, 0, 0)), cols_spec(n),
                   v_f32_spec(n)],
        out_shape=[jax.ShapeDtypeStruct((batch, d, seq), BF16), jax.ShapeDtypeStruct((rows, d), BF16),
                   jax.ShapeDtypeStruct((rows // att_t, n_heads * V_ROWS, att_t), BF16),
                   jax.ShapeDtypeStruct((n, batch, d, keep), F32), jax.ShapeDtypeStruct(v_f32_shape, F32)],
        scratch_shapes=[pltpu.VMEM((d, tm), F32), pltpu.VMEM((d, tm), F32)],
        compiler_params=_params("arbitrary"),
        name="qkv_project_prompt",
    )(*args)


def _lambda_value(lp, lam_init):
    return (jnp.exp(jnp.sum(lp[0:1] * lp[1:2], keepdims=True))
            - jnp.exp(jnp.sum(lp[2:3] * lp[3:4], keepdims=True)) + lam_init)


def _reset_state(m_scr, l_scr, acc_scr):
    m_scr[...] = jnp.full(m_scr.shape, NEG, F32)
    l_scr[...] = jnp.zeros(l_scr.shape, F32)
    acc_scr[...] = jnp.zeros(acc_scr.shape, F32)


def _stack_groups_t(qt):
    feat = lax.broadcasted_iota(jnp.int32, qt.shape, 0)
    zero = jnp.zeros_like(qt)
    low = feat < GROUP
    return jnp.concatenate([jnp.where(low, qt, zero), jnp.where(low, zero, qt)], axis=1)


def _reset_state_t(m_scr, acc_scr):
    m_scr[...] = jnp.full(m_scr.shape, NEG, F32)
    acc_scr[...] = jnp.zeros(acc_scr.shape, F32)


def _update_state_t(m_scr, acc_scr, h, s, vt):
    m = m_scr[h]
    m_new = jnp.maximum(m, jnp.max(s, axis=0, keepdims=True))
    p = jnp.exp2(s - m_new)
    acc_scr[h] = jnp.exp2(m - m_new) * acc_scr[h] + jnp.dot(vt, p.astype(BF16), preferred_element_type=F32)
    m_scr[h] = m_new


def _normalised_t(acc):
    return acc[:LANES] * (1.0 / acc[LANES:LANES + 1])


def _state_scratch_t(n_heads, t):
    return [pltpu.VMEM((n_heads, 1, 2 * t), F32), pltpu.VMEM((n_heads, V_ROWS, 2 * t), F32)]


def _stack_groups(qh):
    lane = lax.broadcasted_iota(jnp.int32, qh.shape, 1)
    zero = jnp.zeros_like(qh)
    low = lane < GROUP
    return jnp.concatenate([jnp.where(low, qh, zero), jnp.where(low, zero, qh)], axis=0)


def _dot_nt(a, b):
    return lax.dot_general(a, b, (((1,), (1,)), ((), ())), preferred_element_type=F32)


def _update_state(m_scr, l_scr, acc_scr, h, s, pv):
    m = m_scr[h]
    m_new = jnp.maximum(m, jnp.max(s, axis=1, keepdims=True))
    alpha = jnp.exp2(m - m_new)
    p = jnp.exp2(s - m_new)
    l_scr[h] = alpha * l_scr[h] + jnp.sum(p, axis=1, keepdims=True)
    acc_scr[h] = alpha * acc_scr[h] + pv(p.astype(BF16))
    m_scr[h] = m_new


def _state_scratch(n_heads, t):
    return [pltpu.VMEM((n_heads, 2 * t, 1), F32), pltpu.VMEM((n_heads, 2 * t, 1), F32),
            pltpu.VMEM((n_heads, 2 * t, LANES), F32)]


def _diff_combine(on, t, lam, sub_gain, lam_init):
    o = on[:t] - lam * on[t:]
    return _rms(o, sub_gain) * (1.0 - lam_init)


def _select_groups(on, t):
    lane = lax.broadcasted_iota(jnp.int32, (t, LANES), 1)
    return jnp.where(lane < GROUP, on[:t], on[t:])


def _diff_prompt_kernel(qt_ref, k_ref, vt_ref, x_ref, wo_ref, sg_ref, lp_ref, o_ref,
                        qs_scr, m_scr, acc_scr, s_scr, mask_scr, obuf, *, lam_init):
    i = pl.program_id(1)
    d, t = qt_ref.shape
    n_heads = d // LANES

    @pl.when((pl.program_id(0) == 0) & (i == 0))
    def _():
        key = lax.broadcasted_iota(jnp.int32, (t, 2 * t), 0)
        qry = lax.broadcasted_iota(jnp.int32, (t, 2 * t), 1) % t
        mask_scr[...] = jnp.where((key // CHUNK) <= (qry // CHUNK), 0.0, NEG)

    for h in range(n_heads):
        qs_scr[h] = _stack_groups_t(qt_ref[_head(h), :])
    _reset_state_t(m_scr, acc_scr)

    def scores(j, slot, diagonal=False):
        rows = pl.ds(pl.multiple_of(j * t, t), t)
        for h in range(n_heads):
            s = jnp.dot(k_ref[rows, _head(h)], qs_scr[h], preferred_element_type=F32)
            s_scr[slot, h] = s + mask_scr[...] if diagonal else s

    def consume(j, slot):
        for h in range(n_heads):
            _update_state_t(m_scr, acc_scr, h, s_scr[slot, h], vt_ref[j, h * V_ROWS:(h + 1) * V_ROWS, :])

    pairs = jnp.maximum(i - 1, 0) // 2
    pl.when(i > 0)(lambda: scores(0, 0))

    def body(p, c):
        j = 2 * p
        scores(j + 1, 1)
        consume(j, 0)
        scores(j + 2, 0)
        consume(j + 1, 1)
        return c

    lax.fori_loop(0, pairs, body, 0)
    left = i - 2 * pairs

    @pl.when(left == 0)
    def _():
        scores(i, 0, True)
        consume(i, 0)

    @pl.when(left == 1)
    def _():
        scores(i, 1, True)
        consume(i - 1, 0)
        consume(i, 1)

    @pl.when(left == 2)
    def _():
        scores(i - 1, 1)
        consume(i - 2, 0)
        scores(i, 0, True)
        consume(i - 1, 1)
        consume(i, 0)

    lam = _lambda_value(lp_ref[...], lam_init)
    for h in range(n_heads):
        on = _normalised_t(acc_scr[h])
        o = on[:, :t] - lam * on[:, t:]
        o = o * lax.rsqrt(jnp.mean(o * o, axis=0, keepdims=True) + EPS)
        obuf[:, _head(h)] = (o.T * sg_ref[:, _head(h)] * (1.0 - lam_init)).astype(BF16)
    o_ref[...] = x_ref[...] + jnp.dot(obuf[...], wo_ref[...], preferred_element_type=F32)


def _diff_attention_prompt(qt, k, vt, x, w_out, sub_gain, lam_p, *, batch, seq, lam_init):
    rows, d = x.shape
    t = vt.shape[2]
    nq = seq // t
    n_heads = d // LANES
    row_spec = pl.BlockSpec((t, d), lambda b, i: (b * nq + i, 0))
    return pl.pallas_call(
        functools.partial(_diff_prompt_kernel, lam_init=lam_init),
        grid=(batch, nq),
        in_specs=[pl.BlockSpec((None, d, t), lambda b, i: (b, 0, i)),
                  pl.BlockSpec((seq, d), lambda b, i: (b, 0)),
                  pl.BlockSpec((nq, n_heads * V_ROWS, t), lambda b, i: (b, 0, 0)),
                  row_spec, _const_spec((d, d)), _const_spec((1, d)), _const_spec(lam_p.shape)],
        out_specs=row_spec,
        out_shape=jax.ShapeDtypeStruct((rows, d), F32),
        scratch_shapes=[pltpu.VMEM((n_heads, LANES, 2 * t), BF16)] + _state_scratch_t(n_heads, t)
        + [pltpu.VMEM((2, n_heads, t, 2 * t), F32), pltpu.VMEM((t, 2 * t), F32), pltpu.VMEM((t, d), BF16)],
        compiler_params=_params("arbitrary", "arbitrary"),
        name="diff_attention_prompt",
    )(qt, k, vt, x, w_out, sub_gain, lam_p)


def _diff_decode_kernel(q_ref, kc_ref, vc_ref, kn_ref, vn_ref, x_ref, wo_ref, sg_ref, lp_ref, o_ref,
                        m_scr, l_scr, acc_scr, s_scr, obuf, *, lam_init):
    j = pl.program_id(1)
    t, d = q_ref.shape
    n_heads = d // LANES
    tk = kc_ref.shape[1]

    @pl.when(j == 0)
    def _():
        _reset_state(m_scr, l_scr, acc_scr)

    for h in range(n_heads):
        qs = _stack_groups(q_ref[:, _head(h)])
        s_scr[h] = jnp.dot(qs, kc_ref[_head(h), :].astype(BF16), preferred_element_type=F32)
    for h in range(n_heads):
        vh = vc_ref[pl.ds(h, tk, stride=n_heads), :].astype(BF16)
        _update_state(m_scr, l_scr, acc_scr, h, s_scr[h], lambda p: jnp.dot(p, vh, preferred_element_type=F32))

    @pl.when(j == pl.num_programs(1) - 1)
    def _():
        lam = _lambda_value(lp_ref[...], lam_init)
        for h in range(n_heads):
            qs = _stack_groups(q_ref[:, _head(h)])
            vh = vn_ref[:, _head(h)]
            _update_state(m_scr, l_scr, acc_scr, h, _dot_nt(qs, kn_ref[:, _head(h)]),
                          lambda p: jnp.dot(p, vh, preferred_element_type=F32))
            on = acc_scr[h] / l_scr[h]
            obuf[:, _head(h)] = _diff_combine(on, t, lam, sg_ref[:, _head(h)], lam_init).astype(BF16)
        o_ref[...] = x_ref[...] + jnp.dot(obuf[...], wo_ref[...], preferred_element_type=F32)


def _diff_attention_decode(q, cache_kt, cache_vh, k_new, v_new, x, w_out, sub_gain, lam_p, *, layer, lam_init):
    _, batch, d, past = cache_kt.shape
    rows = x.shape[0]
    t = rows // batch
    tk = min(DEC_KEY_TILE, past)
    n_heads = d // LANES
    row_spec = pl.BlockSpec((t, d), lambda b, j: (b, 0))
    kt_spec = pl.BlockSpec((None, None, d, tk), lambda b, j: (layer, b, 0, j))
    v_spec = pl.BlockSpec((None, None, tk * n_heads, LANES), lambda b, j: (layer, b, j, 0))
    return pl.pallas_call(
        functools.partial(_diff_decode_kernel, lam_init=lam_init),
        grid=(batch, past // tk),
        in_specs=[row_spec, kt_spec, v_spec, row_spec, row_spec, row_spec, _const_spec((d, d)),
                  _const_spec((1, d)), _const_spec(lam_p.shape)],
        out_specs=row_spec,
        out_shape=jax.ShapeDtypeStruct((rows, d), F32),
        scratch_shapes=_state_scratch(n_heads, t) + [pltpu.VMEM((n_heads, 2 * t, tk), F32), pltpu.VMEM((t, d), BF16)],
        compiler_params=_params("arbitrary", "arbitrary"),
        name="diff_attention_decode",
    )(q, cache_kt, cache_vh, k_new, v_new, x, w_out, sub_gain, lam_p)


def _gather_rows(tbl, idx_of_col, width):
    n_pad = tbl.shape[1]
    t_idx = lax.broadcasted_iota(jnp.int32, (n_pad, width), 0)
    v_idx = lax.broadcasted_iota(jnp.int32, (n_pad, width), 1)
    onehot = jnp.where(t_idx == idx_of_col(v_idx), 1.0, 0.0).astype(BF16)
    hi = tbl.astype(BF16)
    r1 = tbl - hi.astype(F32)
    mid = r1.astype(BF16)
    lo = (r1 - mid.astype(F32)).astype(BF16)
    dot = lambda a: jnp.dot(a, onehot, preferred_element_type=F32)
    return dot(hi) + dot(mid) + dot(lo)


def _rel_index(rel):
    return jnp.clip(rel, -C_REL_CLIP, C_REL_CLIP) + C_REL_CLIP


def _toeplitz(vec, rows, cols, origin):
    width = vec.shape[1]
    xb = jnp.broadcast_to(vec, (rows, width))
    return pltpu.roll(xb, (width - origin) % width, 1, stride=1, stride_axis=0)[:, :cols]


def _bias_kernel(tbl_ref, bp_ref, bs_ref, *, t, n_blocks, ts, win):
    tbl = tbl_ref[...] * LOG2E
    heads = tbl.shape[0]
    key = lax.broadcasted_iota(jnp.int32, (t, t), 0)
    qry = lax.broadcasted_iota(jnp.int32, (t, t), 1)
    for jb in range(n_blocks):
        off = (n_blocks - 1 - jb) * t
        g = _gather_rows(tbl, lambda v: _rel_index(v - (t - 1) + off), 2 * t)
        key_chunk = (jb * t + key) // CHUNK - (n_blocks - 1) * t // CHUNK
        q_chunk = qry // CHUNK
        visible = (key_chunk <= q_chunk) & (key_chunk >= q_chunk - C_LEFT_CHUNKS)
        for h in range(heads):
            tile = _toeplitz(g[h:h + 1, :], t, t, t - 1)
            bp_ref[jb, h // 2, :, (h % 2) * t:(h % 2 + 1) * t] = jnp.where(visible, tile, NEG)
    width = bs_ref.shape[-1]
    wide = 1 << (width + ts).bit_length()
    g = _gather_rows(tbl, lambda v: _rel_index(win + ts - 1 - v), wide)
    for h in range(heads):
        tile = _toeplitz(g[h:h + 1, :], ts, width, ts - 1)
        bs_ref[h // 2, (h % 2) * ts:(h % 2 + 1) * ts, :] = tile


def _band_bias(table_t, *, t, n_blocks, ts, win):
    heads, n = table_t.shape
    n_pad = -(-n // LANES) * LANES
    tbl = jnp.pad(table_t, ((0, 0), (0, n_pad - n)))
    return pl.pallas_call(
        functools.partial(_bias_kernel, t=t, n_blocks=n_blocks, ts=ts, win=win),
        out_shape=[jax.ShapeDtypeStruct((n_blocks, heads // 2, t, 2 * t), F32),
                   jax.ShapeDtypeStruct((heads // 2, 2 * ts, win + ts), F32)],
        compiler_params=pltpu.CompilerParams(vmem_limit_bytes=VMEM_LIMIT_BYTES),
        name="band_bias",
    )(tbl)


def _band_prompt_kernel(qt_ref, k_ref, vt_ref, b_ref, x_ref, wo_ref, o_ref, qs_scr, m_scr, acc_scr, s_scr, obuf):
    i = pl.program_id(1)
    d, t = qt_ref.shape
    n_heads = d // LANES
    n_blocks = b_ref.shape[0]
    first = jnp.maximum(n_blocks - 1 - i, 0)
    for h in range(n_heads):
        qs_scr[h] = _stack_groups_t(qt_ref[_head(h), :])
    _reset_state_t(m_scr, acc_scr)

    def scores(jb):
        rows = pl.ds(pl.multiple_of((i - (n_blocks - 1) + jb) * t, t), t)
        for h in range(n_heads):
            s_scr[jb % 2, h] = (jnp.dot(k_ref[rows, _head(h)], qs_scr[h], preferred_element_type=F32)
                                + b_ref[jb, h])

    def consume(jb):
        j = i - (n_blocks - 1) + jb
        for h in range(n_heads):
            _update_state_t(m_scr, acc_scr, h, s_scr[jb % 2, h], vt_ref[j, h * V_ROWS:(h + 1) * V_ROWS, :])

    def window(f):
        scores(f)
        for jb in range(f, n_blocks):
            if jb + 1 < n_blocks:
                scores(jb + 1)
            consume(jb)

    for f in range(n_blocks):
        pl.when(first == f)(functools.partial(window, f))
    feat = lax.broadcasted_iota(jnp.int32, (LANES, t), 0)
    for h in range(n_heads):
        on = _normalised_t(acc_scr[h])
        obuf[:, _head(h)] = jnp.where(feat < GROUP, on[:, :t], on[:, t:]).T.astype(BF16)
    o_ref[...] = x_ref[...] + jnp.dot(obuf[...], wo_ref[...], preferred_element_type=F32)


def _band_attention_prompt(qt, k, vt, bias, x, w_out, *, batch, seq):
    rows, d = x.shape
    t = vt.shape[2]
    nq = seq // t
    n_heads = d // LANES
    row_spec = pl.BlockSpec((t, d), lambda b, i: (b * nq + i, 0))
    return pl.pallas_call(
        _band_prompt_kernel,
        grid=(batch, nq),
        in_specs=[pl.BlockSpec((None, d, t), lambda b, i: (b, 0, i)),
                  pl.BlockSpec((seq, d), lambda b, i: (b, 0)),
                  pl.BlockSpec((nq, n_heads * V_ROWS, t), lambda b, i: (b, 0, 0)),
                  _const_spec(bias.shape), row_spec, _const_spec((d, d))],
        out_specs=row_spec,
        out_shape=jax.ShapeDtypeStruct((rows, d), F32),
        scratch_shapes=[pltpu.VMEM((n_heads, LANES, 2 * t), BF16)] + _state_scratch_t(n_heads, t)
        + [pltpu.VMEM((2, n_heads, t, 2 * t), F32), pltpu.VMEM((t, d), BF16)],
        compiler_params=_params("arbitrary", "arbitrary"),
        name="band_attention_prompt",
    )(qt, k, vt, bias, x, w_out)


def _band_decode_kernel(q_ref, kc_ref, vc_ref, kn_ref, vn_ref, b_ref, x_ref, wo_ref, o_ref, obuf):
    t, d = q_ref.shape
    win = kc_ref.shape[1]
    for h in range(d // LANES):
        qs = _stack_groups(q_ref[:, _head(h)])
        bias = b_ref[h]
        s_c = jnp.dot(qs, kc_ref[_head(h), :].astype(BF16), preferred_element_type=F32) + bias[:, :win]
        s_n = _dot_nt(qs, kn_ref[:, _head(h)]) + bias[:, win:]
        m = jnp.maximum(jnp.max(s_c, axis=1, keepdims=True), jnp.max(s_n, axis=1, keepdims=True))
        p_c = jnp.exp2(s_c - m)
        p_n = jnp.exp2(s_n - m)
        l = jnp.sum(p_c, axis=1, keepdims=True) + jnp.sum(p_n, axis=1, keepdims=True)
        acc = (_dot_nt(p_c.astype(BF16), vc_ref[_head(h), :].astype(BF16))
               + jnp.dot(p_n.astype(BF16), vn_ref[:, _head(h)], preferred_element_type=F32))
        obuf[:, _head(h)] = _select_groups(acc / l, t).astype(BF16)
    o_ref[...] = x_ref[...] + jnp.dot(obuf[...], wo_ref[...], preferred_element_type=F32)


def _band_attention_decode(q, cache_kt, cache_vt, k_new, v_new, bias, x, w_out, *, layer):
    _, batch, d, win = cache_kt.shape
    rows = x.shape[0]
    t = rows // batch
    row_spec = pl.BlockSpec((t, d), lambda b: (b, 0))
    cache_spec = pl.BlockSpec((None, None, d, win), lambda b: (layer, b, 0, 0))
    return pl.pallas_call(
        _band_decode_kernel,
        grid=(batch,),
        in_specs=[row_spec, cache_spec, cache_spec, row_spec, row_spec, _const_spec(bias.shape), row_spec,
                  _const_spec((d, d))],
        out_specs=row_spec,
        out_shape=jax.ShapeDtypeStruct((rows, d), F32),
        scratch_shapes=[pltpu.VMEM((t, d), BF16)],
        compiler_params=_params("arbitrary"),
        name="band_attention_decode",
    )(q, cache_kt, cache_vt, k_new, v_new, bias, x, w_out)


def _gelu_tanh(x):
    c = -2.0 * math.sqrt(2.0 / math.pi) * LOG2E
    return x / (1.0 + jnp.exp2(x * (c + (0.044715 * c) * (x * x))))


def _rglru_kernel(x_ref, g_ref, win_ref, bin_ref, cw_ref, cb_ref, gaw_ref, gab_ref, gxw_ref, gxb_ref,
                  lam_ref, wout_ref, hist_ref, h0_ref, o_ref, cst_ref, hst_ref, ubuf, hcar, h_scr):
    tm, w = h_scr.shape
    pad = SUBLANES

    @pl.when(pl.program_id(1) == 0)
    def _():
        ubuf[...] = hist_ref[...]
        hcar[...] = h0_ref[...]

    x = x_ref[...]
    xn = _rms(x, g_ref[...]).astype(BF16)
    gu = jnp.dot(xn, win_ref[...], preferred_element_type=F32) + bin_ref[...]
    gate = _gelu_tanh(gu[:, :w])
    u = gu[:, w:]
    cw = cw_ref[...]
    prev = ubuf[...]
    sub8 = lax.broadcasted_iota(jnp.int32, (pad, w), 0)
    xc = cb_ref[...] + cw[3:4] * u
    for j in range(B_CONV_W - 1):
        back = B_CONV_W - 1 - j
        rolled = pltpu.roll(u, back, 0)
        head = jnp.where(sub8 < back, pltpu.roll(prev, back, 0), rolled[0:pad])
        xc = xc + cw[j:j + 1] * jnp.concatenate([head, rolled[pad:]], axis=0)

    xcb = xc.astype(BF16)
    bw = w // B_BLOCKS
    r_parts, i_parts = [], []
    for n in range(B_BLOCKS):
        blk = xcb[:, n * bw:(n + 1) * bw]
        r_parts.append(jnp.dot(blk, gaw_ref[n], preferred_element_type=F32))
        i_parts.append(jnp.dot(blk, gxw_ref[n], preferred_element_type=F32))
    r = jax.nn.sigmoid(jnp.concatenate(r_parts, axis=1) + gab_ref[...])
    ig = jax.nn.sigmoid(jnp.concatenate(i_parts, axis=1) + gxb_ref[...])
    z = -lam_ref[...]
    softplus = jnp.maximum(z, 0.0) + jnp.log1p(jnp.exp(-jnp.abs(z)))
    a = jnp.exp(-B_C * r * softplus)
    b = jnp.sqrt(1.0 - a * a) * (ig * xc)

    a = a.reshape(tm // SUBLANES, SUBLANES, w)
    b = b.reshape(tm // SUBLANES, SUBLANES, w)
    sub = lax.broadcasted_iota(jnp.int32, (1, SUBLANES, w), 1)
    for s in (1, 2, 4):
        a_prev = pltpu.roll(a, s, 1)
        b_prev = pltpu.roll(b, s, 1)
        inside = sub >= s
        b = jnp.where(inside, a * b_prev + b, b)
        a = jnp.where(inside, a * a_prev, a)
    a = a.reshape(tm, w)
    b = b.reshape(tm, w)
    h_prev = hcar[...]
    for gidx in range(tm // SUBLANES):
        rows = slice(gidx * SUBLANES, (gidx + 1) * SUBLANES)
        hg = a[rows] * h_prev + b[rows]
        h_scr[rows, :] = hg
        h_prev = jnp.broadcast_to(hg[SUBLANES - 1:SUBLANES, :], (SUBLANES, w))
    hcar[...] = h_prev
    hs = h_scr[...]

    y = jnp.dot((hs * gate).astype(BF16), wout_ref[...], preferred_element_type=F32)
    o_ref[...] = x + y
    ubuf[...] = u[tm - pad:tm, :]
    cst_ref[...] = u[tm - pad:tm, :]
    hst_ref[...] = h_prev


def _rglru(x, g, p, hist, h0, *, batch, seq):
    rows, d = x.shape
    w = p["w_out"].shape[0]
    tm = _row_tile(seq)
    nt = seq // tm
    row_spec = pl.BlockSpec((tm, d), lambda b, t: (b * nt + t, 0))
    state_spec = pl.BlockSpec((None, SUBLANES, w), lambda b, t: (b, 0, 0))
    consts = [g, p["w_in"], p["b_in"], p["conv_w"], p["conv_b"], p["ga_w"], p["ga_b"], p["gx_w"], p["gx_b"],
              p["lam"], p["w_out"]]
    return pl.pallas_call(
        _rglru_kernel,
        grid=(batch, nt),
        in_specs=[row_spec] + [_const_spec(c.shape) for c in consts] + [state_spec, state_spec],
        out_specs=[row_spec, state_spec, state_spec],
        out_shape=[jax.ShapeDtypeStruct((rows, d), F32), jax.ShapeDtypeStruct((batch, SUBLANES, w), F32),
                   jax.ShapeDtypeStruct((batch, SUBLANES, w), F32)],
        scratch_shapes=[pltpu.VMEM((SUBLANES, w), F32), pltpu.VMEM((SUBLANES, w), F32), pltpu.VMEM((tm, w), F32)],
        compiler_params=_params("arbitrary", "arbitrary"),
        name="rglru_block",
    )(x, *consts, hist, h0)


def _mlp_kernel(x_ref, g_ref, w1_ref, w2_ref, gf_ref, o_ref, *, final_norm):
    x = x_ref[...]
    xn = _rms(x, g_ref[...]).astype(BF16)
    acc = x
    for c in range(w1_ref.shape[1] // FF_CHUNK):
        cols = slice(c * FF_CHUNK, (c + 1) * FF_CHUNK)
        h = jnp.dot(xn, w1_ref[:, cols], preferred_element_type=F32)
        h = jnp.square(jnp.maximum(h, 0.0)).astype(BF16)
        acc = acc + jnp.dot(h, w2_ref[cols, :], preferred_element_type=F32)
    if final_norm:
        acc = _rms(acc, gf_ref[...])
    o_ref[...] = acc


def _mlp(x, g, w1, w2, g_final, *, final_norm):
    rows, d = x.shape
    tm = _row_tile(rows, MLP_TILE)
    row_spec = pl.BlockSpec((tm, d), lambda i: (i, 0))
    return pl.pallas_call(
        functools.partial(_mlp_kernel, final_norm=final_norm),
        grid=(rows // tm,),
        in_specs=[row_spec, _const_spec((1, d)), _const_spec(w1.shape), _const_spec(w2.shape),
                  _const_spec((1, d))],
        out_specs=row_spec,
        out_shape=jax.ShapeDtypeStruct((rows, d), F32),
        compiler_params=_params("arbitrary"),
        name="mlp",
    )(x, g, w1, w2, g_final)


def _rope_angles(pos):
    half = GROUP // 2
    inv = ROPE_THETA ** (-jnp.arange(half, dtype=F32) / half)
    return pos.astype(F32)[:, None] * inv[None, :]


def _rope_tables_rows(pos):
    ang = _rope_angles(pos)
    cos = jnp.tile(jnp.cos(ang), (1, LANES // ang.shape[1]))
    sin = jnp.tile(jnp.concatenate([-jnp.sin(ang), jnp.sin(ang)], axis=1), (1, LANES // GROUP))
    return cos, sin


def _rope_tables_cols(pos):
    ang = _rope_angles(pos).T
    return jnp.cos(ang), jnp.sin(ang)


def _feature_major(cache):
    n, batch, pos = cache.shape[:3]
    nd = cache.ndim
    return jnp.transpose(cache, (0, 1) + tuple(range(3, nd)) + (2,)).reshape(n, batch, -1, pos)


def _position_major(x, feature_dims):
    n, batch, _, pos = x.shape
    nf = len(feature_dims)
    return jnp.transpose(x.reshape(n, batch, *feature_dims, pos), (0, 1, 2 + nf) + tuple(range(2, 2 + nf)))


def kernel(x_prompt, x_sample, cache_a_k, cache_a_v, state_b_conv, state_b_h, cache_c_k, cache_c_v, norm_mix_g, norm_mlp_g, norm_final_g, a_w_in, a_q_norm_g, a_k_norm_g, a_lambda, a_subln_g, a_w_out, b_w_in, b_b_in, b_conv_w, b_conv_b, b_gate_a_w, b_gate_a_b, b_gate_x_w, b_gate_x_b, b_lambda, b_w_out, c_w_in, c_q_norm_g, c_k_norm_g, c_rel_bias, c_w_out, mlp_w1, mlp_w2):
    batch, seq, d = x_prompt.shape
    dbatch, dseq, _ = x_sample.shape
    depth = norm_mix_g.shape[0]
    past = cache_a_k.shape[2]
    n_heads = d // LANES
    xp = x_prompt.reshape(batch * seq, d)
    xs = x_sample.reshape(dbatch * dseq, d)
    row = lambda v: v.reshape(1, -1).astype(F32)
    tile_gain = lambda v: jnp.tile(v.astype(F32), d // v.shape[0]).reshape(1, d)
    grp = jnp.arange(LANES) // GROUP
    bd = (grp[:, None] == grp[None, :]).astype(BF16)
    col_gain = lambda v: jnp.broadcast_to(tile_gain(v).reshape(d, 1), (d, min(ROW_TILE, seq)))
    rope_p = _rope_tables_cols(jnp.arange(seq))
    rope_s = tuple(jnp.tile(t, (dbatch, 1)) for t in _rope_tables_rows(past + jnp.arange(dseq)))
    keep = min(C_BAND_PAST, seq)
    att_t = min(ATT_TILE, seq)
    n_band_blocks = C_BAND_PAST // att_t + 1
    a_kt = _feature_major(cache_a_k)
    a_vh = cache_a_v.reshape(cache_a_v.shape[0], dbatch, past * n_heads, LANES)
    c_kt, c_vt = _feature_major(cache_c_k), _feature_major(cache_c_v)
    c_heads = c_rel_bias.shape[2]

    outs = {name: [] for name in ("aks", "avs", "bcp", "bhp", "bcs", "bhs", "cks", "cvs")}
    akp = avp = ckp = cvp = None
    for layer in range(depth):
        kind, idx = layer % 3, layer // 3
        g_mix = row(norm_mix_g[layer])
        if kind == 0:
            lam_init = 0.8 - 0.6 * math.exp(-0.3 * layer)
            w_in = a_w_in[idx].astype(BF16)
            w_out = a_w_out[idx].astype(BF16)
            qg, kg, sg = tile_gain(a_q_norm_g[idx]), tile_gain(a_k_norm_g[idx]), tile_gain(a_subln_g[idx])
            lam_p = a_lambda[idx].astype(F32)
            qt, k, vt, akp, avp = _project_prompt(xp, g_mix, w_in.T, col_gain(a_q_norm_g[idx]),
                                                  col_gain(a_k_norm_g[idx]), batch=batch, seq=seq, keep=seq,
                                                  rope_tables=rope_p, vf_layout="heads", prev_k=akp, prev_v=avp)
            xp = _diff_attention_prompt(qt, k, vt, xp, w_out, sg, lam_p, batch=batch, seq=seq, lam_init=lam_init)
            q, k, v, kf, vf = _project_rows(xs, g_mix, w_in, qg, kg, bd, rope_tables=rope_s)
            xs = _diff_attention_decode(q, a_kt, a_vh, k, v, xs, w_out, sg, lam_p, layer=idx, lam_init=lam_init)
            outs["aks"].append(kf.reshape(dbatch, dseq, n_heads, 2, GROUP))
            outs["avs"].append(vf.reshape(dbatch, dseq, n_heads, LANES))
        elif kind == 1:
            w = b_w_out.shape[1]
            p = dict(w_in=b_w_in[idx].astype(BF16), b_in=row(b_b_in[idx]), conv_w=b_conv_w[idx].astype(F32),
                     conv_b=row(b_conv_b[idx]), ga_w=b_gate_a_w[idx].astype(BF16), ga_b=row(b_gate_a_b[idx]),
                     gx_w=b_gate_x_w[idx].astype(BF16), gx_b=row(b_gate_x_b[idx]), lam=row(b_lambda[idx]),
                     w_out=b_w_out[idx].astype(BF16))
            zeros = jnp.zeros((batch, SUBLANES, w), F32)
            xp, cst, hst = _rglru(xp, g_mix, p, zeros, zeros, batch=batch, seq=seq)
            outs["bcp"].append(cst[:, SUBLANES - (B_CONV_W - 1):])
            outs["bhp"].append(hst[:, 0])
            hist = jnp.pad(state_b_conv[idx].astype(F32), ((0, 0), (SUBLANES - (B_CONV_W - 1), 0), (0, 0)))
            h0 = jnp.broadcast_to(state_b_h[idx].astype(F32)[:, None, :], (dbatch, SUBLANES, w))
            xs, cst, hst = _rglru(xs, g_mix, p, hist, h0, batch=dbatch, seq=dseq)
            outs["bcs"].append(cst[:, SUBLANES - (B_CONV_W - 1):])
            outs["bhs"].append(hst[:, 0])
        else:
            w_in = c_w_in[idx].astype(BF16)
            w_out = c_w_out[idx].astype(BF16)
            qg, kg = tile_gain(c_q_norm_g[idx]), tile_gain(c_k_norm_g[idx])
            win = c_kt.shape[3]
            bias_p, bias_s = _band_bias(c_rel_bias[idx].astype(F32).T, t=att_t, n_blocks=n_band_blocks, ts=dseq,
                                        win=win)
            qt, k, vt, ckp, cvp = _project_prompt(xp, g_mix, w_in.T, col_gain(c_q_norm_g[idx]),
                                                  col_gain(c_k_norm_g[idx]), batch=batch, seq=seq, keep=keep,
                                                  prev_k=ckp, prev_v=cvp)
            xp = _band_attention_prompt(qt, k, vt, bias_p, xp, w_out, batch=batch, seq=seq)
            q, k, v, kf, vf = _project_rows(xs, g_mix, w_in, qg, kg, bd)
            xs = _band_attention_decode(q, c_kt, c_vt, k, v, bias_s, xs, w_out, layer=idx)
            outs["cks"].append(kf.reshape(dbatch, dseq, c_heads, GROUP))
            outs["cvs"].append(vf.reshape(dbatch, dseq, c_heads, GROUP))
        g_mlp = row(norm_mlp_g[layer])
        w1 = mlp_w1[layer].astype(BF16)
        w2 = mlp_w2[layer].astype(BF16)
        last = layer == depth - 1
        xp = _mlp(xp, g_mlp, w1, w2, row(norm_final_g), final_norm=last)
        xs = _mlp(xs, g_mlp, w1, w2, row(norm_final_g), final_norm=last)
    st = lambda name: jnp.stack(outs[name])
    return (xp.reshape(batch, seq, d), xs.reshape(dbatch, dseq, d),
            _position_major(akp, (n_heads, 2, GROUP)), avp.reshape(-1, batch, seq, n_heads, LANES),
            st("aks"), st("avs"),
            st("bcp"), st("bhp"), st("bcs"), st("bhs"),
            _position_major(ckp, (c_heads, GROUP)), _position_major(cvp, (c_heads, GROUP)),
            st("cks"), st("cvs"))
```
